```python
import math
import jax, jax.numpy as jnp
from jax import lax
import numpy as np

D_MODEL = 1024
BATCH = 8
SEQ = 4096
DEPTH = 2

D_MIX = D_MODEL
RWKV_WIDTH = D_MIX // 4
RWKV_HEAD_DIM = 64
RWKV_HEADS = RWKV_WIDTH // RWKV_HEAD_DIM
RWKV_DECAY_RANK = 64
RWKV_A_RANK = 64
RWKV_V_RANK = 32
RWKV_GATE_RANK = 128
RWKV_GN_EPS = 64e-5
S5_WIDTH = D_MIX // 4
S5_GROUP_DIM = 16
S5_GROUPS = S5_WIDTH // S5_GROUP_DIM
S5_STATE = 64
S5_DT_MIN = 0.001
S5_DT_MAX = 0.1
GDN_WIDTH = D_MIX - RWKV_WIDTH - S5_WIDTH
GDN_HEAD_DIM = 128
GDN_HEADS = GDN_WIDTH // GDN_HEAD_DIM
GDN_CONV = 4
GDN_CHUNK = 64
D_FF = 2816
RMS_EPS = 1e-6
L2_EPS = 1e-12

RWKV_SPLITS = (RWKV_WIDTH, RWKV_WIDTH, RWKV_WIDTH, RWKV_DECAY_RANK, RWKV_A_RANK, RWKV_GATE_RANK)
RWKV_COLS = 3 * RWKV_WIDTH + RWKV_DECAY_RANK + RWKV_A_RANK + RWKV_GATE_RANK
S5_COLS = S5_WIDTH
GDN_COLS = 4 * GDN_WIDTH + 2 * GDN_HEADS
P_MAIN = RWKV_COLS + S5_COLS + GDN_COLS

kernel_name = 'hybrid_rwkv7_s5_gdn_macaron_sandwich'


def _split(z, sizes):
    idx = np.cumsum(sizes)[:-1].tolist()
    return jnp.split(z, idx, axis=-1)


def rms_norm(z, gain):
    zf = z.astype(jnp.float32)
    zf = zf * lax.rsqrt(jnp.mean(zf * zf, axis=-1, keepdims=True) + RMS_EPS)
    return (zf * gain.astype(jnp.float32)).astype(z.dtype)


def l2_normalize(z):
    return z / jnp.maximum(jnp.sqrt(jnp.sum(z * z, axis=-1, keepdims=True)), L2_EPS)


def swiglu(z, w1, w3, w2):
    return (jax.nn.silu(z @ w1) * (z @ w3)) @ w2


def token_shift(z):
    return jnp.pad(z[:, :-1], ((0, 0), (1, 0), (0, 0)))


def rwkv7_time_mix(cols, vres_cols, v_first, mu, w0, w_w2, a0, w_a2, w_g2, k_k, k_a, r_k, ln_w, ln_b, v0, w_v2):
    bsz, t_len, _ = cols.shape
    hn = (RWKV_HEADS, RWKV_HEAD_DIM)
    z = cols + mu * (token_shift(cols) - cols)
    r, k, v, wd, ad, gd = _split(z, RWKV_SPLITS)
    w_log = -jax.nn.softplus(-(w0 + jnp.tanh(wd) @ w_w2)) - 0.5
    decay = jnp.exp(-jnp.exp(w_log))
    a = jax.nn.sigmoid(a0 + ad @ w_a2)
    g = jax.nn.sigmoid(gd) @ w_g2
    if v_first is None:
        v_first = v
    else:
        v = v + (v_first - v) * jax.nn.sigmoid(v0 + vres_cols @ w_v2)
    heads = lambda t: t.reshape(bsz, t_len, RWKV_HEADS, RWKV_HEAD_DIM)
    r, k, v, decay, a = heads(r), heads(k), heads(v), heads(decay), heads(a)
    kk = l2_normalize(k * k_k.reshape(hn))
    k = k * (1.0 + (a - 1.0) * k_a.reshape(hn))

    def step(state, inp):
        r_t, w_t, k_t, v_t, nkk_t, b_t = inp
        sa = jnp.einsum('bhvk,bhk->bhv', state, nkk_t)
        state = (state * w_t[:, :, None, :] + sa[..., None] * b_t[:, :, None, :]
                 + v_t[..., None] * k_t[:, :, None, :])
        return state, jnp.einsum('bhvk,bhk->bhv', state, r_t)

    tm = lambda t: jnp.moveaxis(t, 1, 0)
    s0 = jnp.zeros((bsz, RWKV_HEADS, RWKV_HEAD_DIM, RWKV_HEAD_DIM), jnp.float32)
    _, y = lax.scan(step, s0, (tm(r), tm(decay), tm(k), tm(v), tm(-kk), tm(kk * a)))
    y = jnp.moveaxis(y, 0, 1)
    mean = jnp.mean(y, axis=-1, keepdims=True)
    var = jnp.mean(jnp.square(y - mean), axis=-1, keepdims=True)
    y = (y - mean) * lax.rsqrt(var + RWKV_GN_EPS) * ln_w.reshape(hn) + ln_b.reshape(hn)
    y = y + jnp.sum(r * k * r_k, axis=-1, keepdims=True) * v
    return y.reshape(bsz, t_len, RWKV_WIDTH) * g, v_first


def _complex_linear_combine(e1, e2):
    a1r, a1i, b1r, b1i = e1
    a2r, a2i, b2r, b2i = e2
    return (a2r * a1r - a2i * a1i, a2r * a1i + a2i * a1r,
            a2r * b1r - a2i * b1i + b2r, a2r * b1i + a2i * b1r + b2i)


def s5_mix(u, a_re, a_im, log_dt, b_re, b_im, c_re, c_im, d_skip, w_glu, b_glu):
    bsz, t_len, _ = u.shape
    a_re, a_im, log_dt = a_re.astype(jnp.float32), a_im.astype(jnp.float32), log_dt.astype(jnp.float32)
    ug = u.reshape(bsz, t_len, S5_GROUPS, S5_GROUP_DIM)
    dt = jnp.exp(log_dt)[:, None]
    mag = jnp.exp(dt * a_re)
    ang = dt * a_im
    abar_re, abar_im = mag * jnp.cos(ang), mag * jnp.sin(ang)
    den = a_re * a_re + a_im * a_im
    num_re = abar_re - 1.0
    coef_re = (num_re * a_re + abar_im * a_im) / den
    coef_im = (abar_im * a_re - num_re * a_im) / den
    bbar_re = coef_re[..., None] * b_re - coef_im[..., None] * b_im
    bbar_im = coef_re[..., None] * b_im + coef_im[..., None] * b_re
    bu_re = jnp.einsum('btgh,gnh->btgn', ug, bbar_re)
    bu_im = jnp.einsum('btgh,gnh->btgn', ug, bbar_im)
    shape = (1, t_len, S5_GROUPS, S5_STATE)
    elems = (jnp.broadcast_to(abar_re, shape), jnp.broadcast_to(abar_im, shape), bu_re, bu_im)
    _, _, x_re, x_im = lax.associative_scan(_complex_linear_combine, elems, axis=1)
    y = (jnp.einsum('btgn,ghn->btgh', x_re, c_re) - jnp.einsum('btgn,ghn->btgh', x_im, c_im)
         + d_skip.reshape(S5_GROUPS, S5_GROUP_DIM) * ug)
    z = jax.nn.gelu(y.reshape(bsz, t_len, S5_WIDTH))
    return z * jax.nn.sigmoid(z @ w_glu + b_glu)


def causal_depthwise_conv(z, w):
    width, ch = w.shape
    return lax.conv_general_dilated(z, w.astype(z.dtype)[:, None, :], window_strides=(1,),
                                    padding=((width - 1, 0),), dimension_numbers=('NWC', 'WIO', 'NWC'),
                                    feature_group_count=ch)


def chunk_gated_delta_rule(q, k, v, g, beta):
    bsz, t_len, nh, dk = q.shape
    dv = v.shape[-1]
    n_chunks = t_len // GDN_CHUNK
    chunks = lambda t: jnp.moveaxis(t.reshape(bsz, n_chunks, GDN_CHUNK, nh, -1), 3, 1)
    q, k, v = chunks(q), chunks(k), chunks(v)
    g = chunks(g[..., None])[..., 0]
    beta = chunks(beta[..., None])[..., 0]
    gam = jnp.cumsum(g, axis=-1)
    causal = jnp.tril(jnp.ones((GDN_CHUNK, GDN_CHUNK), dtype=bool))
    strict = jnp.tril(jnp.ones((GDN_CHUNK, GDN_CHUNK), dtype=bool), k=-1)
    decay = jnp.exp(jnp.where(causal, gam[..., :, None] - gam[..., None, :], -jnp.inf))
    k_beta = k * beta[..., None]
    a_low = jnp.where(strict, jnp.einsum('bhncd,bhnsd->bhncs', k_beta, k) * decay, 0.0)
    rhs = jnp.concatenate([v * beta[..., None], k_beta * jnp.exp(gam)[..., None]], axis=-1)
    eye = jnp.eye(GDN_CHUNK, dtype=a_low.dtype)
    sol = lax.linalg.triangular_solve(a_low + eye, rhs, left_side=True, lower=True, unit_diagonal=True)
    u, w = sol[..., :dv], sol[..., dv:]
    attn = jnp.where(causal, jnp.einsum('bhncd,bhnsd->bhncs', q, k) * decay, 0.0)
    q_dec = q * jnp.exp(gam)[..., None]
    k_dec = k * jnp.exp(gam[..., -1:] - gam)[..., None]
    chunk_decay = jnp.exp(gam[..., -1])

    def step(state, inp):
        u_c, w_c, q_c, k_c, attn_c, dec_c = inp
        v_new = u_c - jnp.einsum('bhck,bhkv->bhcv', w_c, state)
        out = jnp.einsum('bhck,bhkv->bhcv', q_c, state) + jnp.einsum('bhcs,bhsv->bhcv', attn_c, v_new)
        state = state * dec_c[..., None, None] + jnp.einsum('bhck,bhcv->bhkv', k_c, v_new)
        return state, out

    along = lambda t: jnp.moveaxis(t, 2, 0)
    s0 = jnp.zeros((bsz, nh, dk, dv), jnp.float32)
    _, o = lax.scan(step, s0, (along(u), along(w), along(q_dec), along(k_dec), along(attn), along(chunk_decay)))
    o = jnp.moveaxis(o, 0, 2)
    return jnp.moveaxis(o, 1, 3).reshape(bsz, t_len, nh, dv)


def gated_deltanet_mix(cols, conv_w, a_log, dt_bias, norm_w):
    bsz, t_len, _ = cols.shape
    qkv, gate, beta_logit, alpha_logit = _split(cols, (3 * GDN_WIDTH, GDN_WIDTH, GDN_HEADS, GDN_HEADS))
    qkv = jax.nn.silu(causal_depthwise_conv(qkv, conv_w))
    heads = lambda t: t.reshape(bsz, t_len, GDN_HEADS, GDN_HEAD_DIM)
    q, k, v = [heads(t) for t in _split(qkv, (GDN_WIDTH, GDN_WIDTH, GDN_WIDTH))]
    q = l2_normalize(q) * (GDN_HEAD_DIM ** -0.5)
    k = l2_normalize(k)
    beta = jax.nn.sigmoid(beta_logit)
    g = -jnp.exp(a_log.astype(jnp.float32)) * jax.nn.softplus(alpha_logit + dt_bias)
    o = chunk_gated_delta_rule(q, k, v, g, beta)
    o = o * lax.rsqrt(jnp.mean(o * o, axis=-1, keepdims=True) + RMS_EPS) * norm_w
    o = o * jax.nn.silu(heads(gate))
    return o.reshape(bsz, t_len, GDN_WIDTH)


def setup_inputs(seed: int = 0) -> dict:
    key = jax.random.key(seed)
    ks = iter(jax.random.split(key, 64))
    f32 = jnp.float32

    def nrm(shape, scale):
        return scale * jax.random.normal(next(ks), shape, f32)

    def uni(shape, lo, hi):
        return jax.random.uniform(next(ks), shape, f32, lo, hi)

    L, LV = DEPTH, DEPTH - 1
    W = RWKV_WIDTH
    G, N, HC = S5_GROUPS, S5_STATE, S5_GROUP_DIM
    log_dt_lo, log_dt_hi = math.log(S5_DT_MIN), math.log(S5_DT_MAX)
    ramp = jnp.linspace(0.0, 1.0, W, dtype=f32) ** 0.85
    gdn_dt = jnp.exp(uni((L, GDN_HEADS), log_dt_lo, log_dt_hi))
    return {
        'x': nrm((BATCH, SEQ, D_MODEL), 1.0),
        'norm_gain': 1.0 + nrm((L, 6, D_MODEL), 0.02),
        'ffn_w1': nrm((L, 2, D_MODEL, D_FF), D_MODEL ** -0.5),
        'ffn_w3': nrm((L, 2, D_MODEL, D_FF), D_MODEL ** -0.5),
        'ffn_w2': nrm((L, 2, D_FF, D_MODEL), D_FF ** -0.5),
        'w_in': nrm((L, D_MODEL, P_MAIN), D_MODEL ** -0.5),
        'w_in_vres': nrm((LV, D_MODEL, RWKV_V_RANK), D_MODEL ** -0.5),
        'w_out': nrm((L, D_MIX, D_MODEL), D_MIX ** -0.5),
        'rwkv_mu': uni((L, RWKV_COLS), 0.0, 1.0),
        'rwkv_w0': -5.5 + 5.0 * ramp + nrm((L, W), 0.05),
        'rwkv_w_w2': nrm((L, RWKV_DECAY_RANK, W), 0.1 * RWKV_DECAY_RANK ** -0.5),
        'rwkv_a0': nrm((L, W), 0.1),
        'rwkv_w_a2': nrm((L, RWKV_A_RANK, W), 0.1 * RWKV_A_RANK ** -0.5),
        'rwkv_w_g2': nrm((L, RWKV_GATE_RANK, W), RWKV_GATE_RANK ** -0.5),
        'rwkv_k_k': 0.85 + nrm((L, W), 0.02),
        'rwkv_k_a': 1.0 + nrm((L, W), 0.02),
        'rwkv_r_k': -0.04 + nrm((L, RWKV_HEADS, RWKV_HEAD_DIM), 0.02),
        'rwkv_ln_w': 1.0 + nrm((L, W), 0.02),
        'rwkv_ln_b': nrm((L, W), 0.02),
        'rwkv_v0': 1.0 + nrm((LV, W), 0.1),
        'rwkv_w_v2': nrm((LV, RWKV_V_RANK, W), 0.1 * RWKV_V_RANK ** -0.5),
        's5_a_re': -0.5 + nrm((L, G, N), 0.01),
        's5_a_im': jnp.broadcast_to(jnp.pi * jnp.arange(N, dtype=f32), (L, G, N)),
        's5_log_dt': uni((L, G), log_dt_lo, log_dt_hi),
        's5_b_re': nrm((L, G, N, HC), (2 * HC) ** -0.5),
        's5_b_im': nrm((L, G, N, HC), (2 * HC) ** -0.5),
        's5_c_re': nrm((L, G, HC, N), (2 * N) ** -0.5),
        's5_c_im': nrm((L, G, HC, N), (2 * N) ** -0.5),
        's5_d': nrm((L, S5_WIDTH), 1.0),
        's5_w_glu': nrm((L, S5_WIDTH, S5_WIDTH), S5_WIDTH ** -0.5),
        's5_b_glu': nrm((L, S5_WIDTH), 0.02),
        'gdn_conv_w': nrm((L, GDN_CONV, 3 * GDN_WIDTH), GDN_CONV ** -0.5),
        'gdn_a_log': jnp.log(uni((L, GDN_HEADS), 1.0, 16.0)),
        'gdn_dt_bias': gdn_dt + jnp.log(-jnp.expm1(-gdn_dt)),
        'gdn_norm_w': 1.0 + nrm((L, GDN_HEAD_DIM), 0.02),
    }


def reference(x, norm_gain, ffn_w1, ffn_w3, ffn_w2, w_in, w_in_vres, w_out,
              rwkv_mu, rwkv_w0, rwkv_w_w2, rwkv_a0, rwkv_w_a2, rwkv_w_g2, rwkv_k_k, rwkv_k_a,
              rwkv_r_k, rwkv_ln_w, rwkv_ln_b, rwkv_v0, rwkv_w_v2,
              s5_a_re, s5_a_im, s5_log_dt, s5_b_re, s5_b_im, s5_c_re, s5_c_im, s5_d, s5_w_glu, s5_b_glu,
              gdn_conv_w, gdn_a_log, gdn_dt_bias, gdn_norm_w):
    v_first = None
    for l in range(DEPTH):
        ff = swiglu(rms_norm(x, norm_gain[l, 0]), ffn_w1[l, 0], ffn_w3[l, 0], ffn_w2[l, 0])
        x = x + 0.5 * rms_norm(ff, norm_gain[l, 1])

        h = rms_norm(x, norm_gain[l, 2])
        w_proj = w_in[l] if l == 0 else jnp.concatenate([w_in[l], w_in_vres[l - 1]], axis=1)
        p = (h @ w_proj).astype(jnp.float32)
        rw_cols = p[..., :RWKV_COLS]
        s5_cols = p[..., RWKV_COLS:RWKV_COLS + S5_COLS]
        gdn_cols = p[..., RWKV_COLS + S5_COLS:P_MAIN]
        vres_cols = None if l == 0 else p[..., P_MAIN:]
        v0 = None if l == 0 else rwkv_v0[l - 1]
        w_v2 = None if l == 0 else rwkv_w_v2[l - 1]

        y_a, v_first = rwkv7_time_mix(rw_cols, vres_cols, v_first, rwkv_mu[l], rwkv_w0[l], rwkv_w_w2[l],
                                      rwkv_a0[l], rwkv_w_a2[l], rwkv_w_g2[l], rwkv_k_k[l], rwkv_k_a[l],
                                      rwkv_r_k[l], rwkv_ln_w[l], rwkv_ln_b[l], v0, w_v2)
        y_b = s5_mix(s5_cols, s5_a_re[l], s5_a_im[l], s5_log_dt[l], s5_b_re[l], s5_b_im[l],
                     s5_c_re[l], s5_c_im[l], s5_d[l], s5_w_glu[l], s5_b_glu[l])
        y_c = gated_deltanet_mix(gdn_cols, gdn_conv_w[l], gdn_a_log[l], gdn_dt_bias[l], gdn_norm_w[l])
        mixed = jnp.concatenate([y_a, y_b, y_c], axis=-1).astype(x.dtype) @ w_out[l]
        x = x + rms_norm(mixed, norm_gain[l, 3])

        ff = swiglu(rms_norm(x, norm_gain[l, 4]), ffn_w1[l, 1], ffn_w3[l, 1], ffn_w2[l, 1])
        x = x + 0.5 * rms_norm(ff, norm_gain[l, 5])
    return x
```

```python
import functools
import math

import jax
import jax.numpy as jnp
from jax import lax
from jax.experimental import pallas as pl
from jax.experimental.pallas import tpu as pltpu

F32 = jnp.float32
BF16 = jnp.bfloat16

D_MODEL = 1024
D_FF = 2816
RMS_EPS = 1e-6
L2_EPS = 1e-12

RW_WIDTH = 256
RW_HEAD = 64
RW_HEADS = 4
RW_GN_EPS = 64e-5
RW_COLS = 1024
RW_V_RANK = 32

S5_WIDTH = 256
S5_GROUPS = 16
S5_GROUP_DIM = 16
S5_STATE = 64
S5_NSTATE = S5_GROUPS * S5_STATE

GDN_WIDTH = 512
GDN_HEAD = 128
GDN_HEADS = 4
GDN_CONV = 4

P_SMALL = 128

CHUNK = 64
MIX_TB = 256
S5_TB = 128
FFN_TM = 512
FFN_TF = 256
VMEM_LIMIT = 56 * 1024 * 1024


def _bdot(a, b):
    return jnp.dot(a.astype(BF16), b.astype(BF16), preferred_element_type=F32)


def _bdot_nt(a, b):
    return lax.dot_general(a.astype(BF16), b.astype(BF16), (((1,), (1,)), ((), ())),
                           preferred_element_type=F32)


def _bdot_tn(a, b):
    return lax.dot_general(a.astype(BF16), b.astype(BF16), (((0,), (0,)), ((), ())),
                           preferred_element_type=F32)


def _split(x, n):
    pieces = []
    r = x
    for i in range(n):
        p = r.astype(BF16)
        pieces.append(p)
        if i + 1 < n:
            r = r - p.astype(F32)
    return pieces


def _dot_exact_rhs(w01, x, n):
    out = None
    for p in _split(x, n):
        t = jnp.dot(w01, p, preferred_element_type=F32)
        out = t if out is None else out + t
    return out


def _dot_exact_lhs(x, w01, n):
    out = None
    for p in _split(x, n):
        t = jnp.dot(p, w01, preferred_element_type=F32)
        out = t if out is None else out + t
    return out


def _rms(z, gain):
    ms = jnp.mean(z * z, axis=-1, keepdims=True)
    return z * lax.rsqrt(ms + RMS_EPS) * gain


def _sigmoid(x):
    return 1.0 / (1.0 + jnp.exp(-x))


def _softplus(x):
    return jnp.maximum(x, 0.0) + jnp.log1p(jnp.exp(-jnp.abs(x)))


def _iota(shape, dim):
    return lax.broadcasted_iota(jnp.int32, shape, dim)


def _chunk_masks(tb):
    r = _iota((tb, tb), 0)
    c = _iota((tb, tb), 1)
    same = (r // CHUNK) == (c // CHUNK)
    lower = jnp.where(same & (c <= r), 1.0, 0.0).astype(BF16)
    ones = jnp.where(same, 1.0, 0.0).astype(BF16)
    return lower, ones


def _block_ones(n, blk):
    r = _iota((n, n), 0)
    c = _iota((n, n), 1)
    return (r // blk) == (c // blk)


def _unit_lower_inverse(ns):
    r = _iota((CHUNK, CHUNK), 0)
    c = _iota((CHUNK, CHUNK), 1)
    eye = jnp.where(r == c, 1.0, 0.0).astype(F32)
    ts = [eye + n for n in ns]
    ps = list(ns)
    for _ in range(int(math.log2(CHUNK)) - 1):
        ps = [_bdot(p, p) for p in ps]
        ts = [t + _bdot(t, p) for t, p in zip(ts, ps)]
    return ts


def _ffn_core(x, gi, go, w1_ref, w3_ref, w2_ref):
    h = _rms(x, gi).astype(BF16)
    acc = jnp.zeros(x.shape, F32)
    for f in range(D_FF // FFN_TF):
        cs = slice(f * FFN_TF, (f + 1) * FFN_TF)
        a = jnp.dot(h, w1_ref[:, cs], preferred_element_type=F32)
        b = jnp.dot(h, w3_ref[:, cs], preferred_element_type=F32)
        act = (a * _sigmoid(a) * b).astype(BF16)
        acc = acc + jnp.dot(act, w2_ref[cs, :], preferred_element_type=F32)
    return x + 0.5 * _rms(acc, go)


def _ffn_body(x_ref, g_ref, w1_ref, w3_ref, w2_ref, o_ref):
    o_ref[...] = _ffn_core(x_ref[...], g_ref[0:1, :], g_ref[1:2, :], w1_ref, w3_ref, w2_ref)


def _mixout_ffn_body(x_ref, ya_ref, yb_ref, yc_ref, g_ref, wo_ref, w1_ref, w3_ref, w2_ref, o_ref):
    mixed = (jnp.dot(ya_ref[...].astype(BF16), wo_ref[0:256, :], preferred_element_type=F32)
             + jnp.dot(yb_ref[...].astype(BF16), wo_ref[256:512, :], preferred_element_type=F32)
             + jnp.dot(yc_ref[...].astype(BF16), wo_ref[512:1024, :], preferred_element_type=F32))
    x = x_ref[...] + _rms(mixed, g_ref[0:1, :])
    o_ref[...] = _ffn_core(x, g_ref[1:2, :], g_ref[2:3, :], w1_ref, w3_ref, w2_ref)


def _const_spec(shape):
    nd = len(shape)
    return pl.BlockSpec(shape, lambda *_: (0,) * nd, pipeline_mode=pl.Buffered(1))


def _ffn(x, gains, w1, w3, w2):
    bsz, t_len, _ = x.shape
    row = pl.BlockSpec((None, FFN_TM, D_MODEL), lambda b, t: (b, t, 0))
    return pl.pallas_call(
        _ffn_body,
        grid=(bsz, t_len // FFN_TM),
        in_specs=[row, _const_spec(gains.shape), _const_spec(w1.shape), _const_spec(w3.shape),
                  _const_spec(w2.shape)],
        out_specs=row,
        out_shape=jax.ShapeDtypeStruct(x.shape, F32),
        compiler_params=pltpu.CompilerParams(dimension_semantics=("parallel", "parallel"),
                                             vmem_limit_bytes=VMEM_LIMIT),
        name="ffn",
    )(x, gains, w1, w3, w2)


def _mixout_ffn(x, ya, yb_tm, yc, gains, wo, w1, w3, w2):
    bsz, t_len, _ = x.shape
    row = lambda w: pl.BlockSpec((None, FFN_TM, w), lambda b, t: (b, t, 0))
    yb_spec = pl.BlockSpec((FFN_TM, S5_WIDTH), lambda b, t: (t, b))
    return pl.pallas_call(
        _mixout_ffn_body,
        grid=(bsz, t_len // FFN_TM),
        in_specs=[row(D_MODEL), row(RW_WIDTH), yb_spec, row(GDN_WIDTH), _const_spec(gains.shape),
                  _const_spec(wo.shape), _const_spec(w1.shape), _const_spec(w3.shape),
                  _const_spec(w2.shape)],
        out_specs=row(D_MODEL),
        out_shape=jax.ShapeDtypeStruct(x.shape, F32),
        compiler_params=pltpu.CompilerParams(dimension_semantics=("parallel", "parallel"),
                                             vmem_limit_bytes=VMEM_LIMIT),
        name="mixout_ffn",
    )(x, ya, yb_tm, yc, gains, wo, w1, w3, w2)


_QKV_W = 3 * GDN_WIDTH
_PROJ_OFF = (0, RW_COLS, RW_COLS + _QKV_W, RW_COLS + _QKV_W + GDN_WIDTH,
             RW_COLS + _QKV_W + GDN_WIDTH + S5_WIDTH)
_PROJ_W = _PROJ_OFF[-1] + P_SMALL


def _inproj_body(x_ref, g_ref, w_ref, rw_ref, qkv_ref, gate_ref, s5_ref, small_ref):
    h = _rms(x_ref[...], g_ref[...]).astype(BF16)
    outs = (rw_ref, qkv_ref, gate_ref, s5_ref, small_ref)
    for i, o_ref in enumerate(outs):
        lo = _PROJ_OFF[i]
        hi = lo + o_ref.shape[-1]
        o_ref[...] = jnp.dot(h, w_ref[:, lo:hi], preferred_element_type=F32)


def _inproj(x, gain, w):
    bsz, t_len, _ = x.shape
    row = lambda wd: pl.BlockSpec((None, FFN_TM, wd), lambda b, t: (b, t, 0))
    sds = lambda wd: jax.ShapeDtypeStruct((bsz, t_len, wd), F32)
    return pl.pallas_call(
        _inproj_body,
        grid=(bsz, t_len // FFN_TM),
        in_specs=[row(D_MODEL), _const_spec(gain.shape), _const_spec(w.shape)],
        out_specs=[row(RW_COLS), row(_QKV_W), row(GDN_WIDTH),
                   pl.BlockSpec((FFN_TM, S5_WIDTH), lambda b, t: (t, b)), row(P_SMALL)],
        out_shape=[sds(RW_COLS), sds(_QKV_W), sds(GDN_WIDTH),
                   jax.ShapeDtypeStruct((t_len, bsz * S5_WIDTH), F32), sds(P_SMALL)],
        compiler_params=pltpu.CompilerParams(dimension_semantics=("parallel", "parallel"),
                                             vmem_limit_bytes=VMEM_LIMIT),
        name="inproj",
    )(x, gain, w)


def _rwkv_body(has_vres, *refs):
    if has_vres:
        (p_ref, small_ref, vfirst_ref, vec_ref, wlr_ref, wg2_ref, wv2_ref,
         y_ref, prev_ref, h_ref) = refs
    else:
        (p_ref, vec_ref, wlr_ref, wg2_ref, y_ref, vout_ref, prev_ref, h_ref) = refs
    tb = MIX_TB
    width = RW_WIDTH

    @pl.when(pl.program_id(1) == 0)
    def _():
        prev_ref[...] = jnp.zeros(prev_ref.shape, F32)
        h_ref[...] = jnp.zeros(h_ref.shape, F32)

    vec = lambda i: vec_ref[i:i + 1, :]
    mu = jnp.concatenate([vec(10), vec(11), vec(12), vec(13)], axis=1)

    cols = p_ref[...]
    rolled = pltpu.roll(cols, 1, axis=0)
    first = _iota((tb, 1), 0) == 0
    shifted = jnp.where(first, prev_ref[0:1, :], rolled)
    prev_ref[0:1, :] = cols[tb - 1:tb, :]
    z = cols + mu * (shifted - cols)

    r = z[:, 0:256]
    k = z[:, 256:512]
    v = z[:, 512:768]
    lr_in = z[:, 768:896]
    lr_act = jnp.where(_iota((1, 128), 1) < 64, jnp.tanh(lr_in), lr_in)
    lr = _bdot(lr_act, wlr_ref[...])
    w_log = -_softplus(-(vec(0) + lr[:, 0:256])) - 0.5
    lw = -jnp.exp(w_log)
    a = _sigmoid(vec(1) + lr[:, 256:512])
    g = _bdot(_sigmoid(z[:, 896:1024]), wg2_ref[...])
    if has_vres:
        mix = _sigmoid(vec(8) + _bdot(small_ref[...], wv2_ref[...]))
        v = v + (vfirst_ref[...] - v) * mix
    else:
        vout_ref[...] = v

    head_ones = jnp.where(_block_ones(width, RW_HEAD), 1.0, 0.0).astype(BF16)
    kkr = k * vec(2)
    kk = kkr * lax.rsqrt(jnp.maximum(_dot_exact_lhs(kkr * kkr, head_ones, 2), L2_EPS * L2_EPS))
    k2 = k * (1.0 + (a - 1.0) * vec(3))
    nkk = -kk
    bv = kk * a

    lower, ones = _chunk_masks(tb)
    cl = _dot_exact_rhs(lower, lw, 3)
    cl_end = _dot_exact_rhs(ones, lw, 3)
    e_inv = jnp.exp(-cl)
    e_end = jnp.exp(cl_end - cl)
    rt = r * jnp.exp(cl)
    at = nkk * jnp.exp(cl - lw)
    bt = bv * e_inv
    kt = k2 * e_inv
    bh = bv * e_end
    kh = k2 * e_end
    g_end = jnp.exp(cl_end)

    lane = _iota((1, width), 1)
    hmask = [(lane // RW_HEAD) == h for h in range(RW_HEADS)]
    ri = _iota((CHUNK, 2 * CHUNK), 0)
    ci = _iota((CHUNK, 2 * CHUNK), 1)
    strict = (ci % CHUNK) < ri
    incl = (ci % CHUNK) <= ri
    right = ci >= CHUNK
    bd = _block_ones(width, RW_HEAD)

    def pick(stack, cols_sl):
        out = None
        for h in range(RW_HEADS):
            t = jnp.where(hmask[h], stack[h * CHUNK:(h + 1) * CHUNK, cols_sl], 0.0)
            out = t if out is None else out + t
        return out

    ys = []
    for c in range(tb // CHUNK):
        sl = slice(c * CHUNK, (c + 1) * CHUNK)
        at_c, rt_c, v_c = at[sl], rt[sl], v[sl]
        lhs = jnp.concatenate(
            [jnp.where(hmask[h], x, 0.0).astype(BF16) for h in range(RW_HEADS) for x in (at_c, rt_c)], axis=0)
        rhs = jnp.concatenate([bt[sl], kt[sl]], axis=0)
        p1 = _bdot_nt(lhs, rhs)
        aa = [jnp.where(strict, p1[h * 2 * CHUNK:h * 2 * CHUNK + CHUNK], 0.0) for h in range(RW_HEADS)]
        ar = [jnp.where(incl, p1[h * 2 * CHUNK + CHUNK:(h + 1) * 2 * CHUNK], 0.0) for h in range(RW_HEADS)]
        tinv = _unit_lower_inverse([x[:, 0:CHUNK] for x in aa])
        aak = jnp.concatenate([jnp.where(right, x, 0.0) for x in aa], axis=0)
        x1 = pick(_bdot(aak, jnp.concatenate([v_c, v_c], axis=0)), slice(0, width))
        wu = _bdot(jnp.concatenate(tinv, axis=0), jnp.concatenate([at_c, x1], axis=1))
        wt = pick(wu, slice(0, width))
        ut = pick(wu, slice(width, 2 * width))

        hb = h_ref[...]
        uy = _bdot(jnp.concatenate([wt, rt_c], axis=0), hb)
        u = ut + uy[0:CHUNK]
        uv = jnp.concatenate([u, v_c], axis=0)
        o1 = _bdot(jnp.concatenate(ar, axis=0), uv)
        ys.append(uy[CHUNK:2 * CHUNK] + pick(o1, slice(0, width)))
        o2 = _bdot_tn(jnp.concatenate([bh[sl], kh[sl]], axis=0), uv)
        gcol = jnp.transpose(jnp.concatenate([g_end[sl]] * (width // CHUNK), axis=0))
        h_ref[...] = gcol * hb + jnp.where(bd, o2, 0.0)

    y = jnp.concatenate(ys, axis=0)
    inv_n = 1.0 / RW_HEAD
    mean = _dot_exact_lhs(y, head_ones, 2) * inv_n
    yc = y - mean
    var = _dot_exact_lhs(yc * yc, head_ones, 2) * inv_n
    yn = yc * lax.rsqrt(var + RW_GN_EPS) * vec(5) + vec(6)
    yn = yn + _dot_exact_lhs(r * k2 * vec(4), head_ones, 2) * v
    y_ref[...] = yn * g


def _rwkv(p_rw, small, v_first, vecs, wlr, wg2, wv2):
    bsz, t_len, _ = p_rw.shape
    has_vres = v_first is not None
    row = lambda w: pl.BlockSpec((None, MIX_TB, w), lambda b, t: (b, t, 0))
    y_sds = jax.ShapeDtypeStruct((bsz, t_len, RW_WIDTH), F32)
    if has_vres:
        args = (p_rw, small, v_first, vecs, wlr, wg2, wv2)
        in_specs = [row(RW_COLS), row(P_SMALL), row(RW_WIDTH), _const_spec(vecs.shape),
                    _const_spec(wlr.shape), _const_spec(wg2.shape), _const_spec(wv2.shape)]
        out_specs, out_shape = row(RW_WIDTH), y_sds
    else:
        args = (p_rw, vecs, wlr, wg2)
        in_specs = [row(RW_COLS), _const_spec(vecs.shape), _const_spec(wlr.shape), _const_spec(wg2.shape)]
        out_specs, out_shape = [row(RW_WIDTH), row(RW_WIDTH)], [y_sds, y_sds]
    return pl.pallas_call(
        functools.partial(_rwkv_body, has_vres),
        grid=(bsz, t_len // MIX_TB),
        in_specs=in_specs,
        out_specs=out_specs,
        out_shape=out_shape,
        scratch_shapes=[pltpu.VMEM((8, RW_COLS), F32), pltpu.VMEM((RW_WIDTH, RW_WIDTH), F32)],
        compiler_params=pltpu.CompilerParams(dimension_semantics=("parallel", "arbitrary"),
                                             vmem_limit_bytes=VMEM_LIMIT),
        name="rwkv_vres" if has_vres else "rwkv",
    )(*args)


def _s5_disc_body(are_ref, aim_ref, ldt_ref, bre_ref, bim_ref, abar_re_ref, abar_im_ref, bbre_ref, bbim_ref):
    a_re = are_ref[...]
    a_im = aim_ref[...]
    dt = jnp.exp(ldt_ref[...])
    mag = jnp.exp(dt * a_re)
    ang = dt * a_im
    abar_re = mag * jnp.cos(ang)
    abar_im = mag * jnp.sin(ang)
    den = a_re * a_re + a_im * a_im
    num_re = abar_re - 1.0
    coef_re = (num_re * a_re + abar_im * a_im) / den
    coef_im = (abar_im * a_re - num_re * a_im) / den
    abar_re_ref[...] = abar_re
    abar_im_ref[...] = abar_im
    b_re = bre_ref[...]
    b_im = bim_ref[...]
    bbre_ref[...] = coef_re * b_re - coef_im * b_im
    bbim_ref[...] = coef_re * b_im + coef_im * b_re


def _s5_discretize(a_re, a_im, log_dt, b_re, b_im):
    n = S5_NSTATE
    col = lambda z: z.astype(F32).reshape(n, 1)
    ldt = jnp.broadcast_to(log_dt.astype(F32)[:, None], (S5_GROUPS, S5_STATE))
    mat = lambda z: z.reshape(n, S5_GROUP_DIM)
    return pl.pallas_call(
        _s5_disc_body,
        out_shape=[jax.ShapeDtypeStruct((n, 1), F32)] * 2 + [jax.ShapeDtypeStruct((n, S5_GROUP_DIM), F32)] * 2,
        name="s5_discretize",
    )(col(a_re), col(a_im), col(ldt), mat(b_re), mat(b_im))


def _gelu_tanh(x):
    return 0.5 * x * (1.0 + jnp.tanh(math.sqrt(2.0 / math.pi) * (x + 0.044715 * (x * x * x))))


def _s5_body(nb, u_ref, abar_ref, bmat_ref, cmat_ref, vec_ref, wglu_ref, y_ref, x_ref, st_ref):
    n = S5_NSTATE

    @pl.when(pl.program_id(0) == 0)
    def _():
        st_ref[...] = jnp.zeros(st_ref.shape, F32)

    u = u_ref[...]
    x_ref[...] = _bdot(u, bmat_ref[...])
    a_re = jnp.broadcast_to(abar_ref[0:1, :], (nb, n))
    a_im = jnp.broadcast_to(abar_ref[1:2, :], (nb, n))

    def step(t, carry):
        s_re, s_im = carry
        rows = pl.ds(pl.multiple_of(t * nb, nb), nb)
        bu_re = x_ref[rows, 0:n]
        bu_im = x_ref[rows, n:2 * n]
        n_re = a_re * s_re - a_im * s_im + bu_re
        n_im = a_re * s_im + a_im * s_re + bu_im
        x_ref[rows, 0:n] = n_re
        x_ref[rows, n:2 * n] = n_im
        return n_re, n_im

    s_re, s_im = lax.fori_loop(0, S5_TB, step, (st_ref[:, 0:n], st_ref[:, n:2 * n]))
    st_ref[:, 0:n] = s_re
    st_ref[:, n:2 * n] = s_im

    y = _bdot(x_ref[...], cmat_ref[...]) + vec_ref[0:1, :] * u
    zz = _gelu_tanh(y)
    y_ref[...] = zz * _sigmoid(_bdot(zz, wglu_ref[...]) + vec_ref[1:2, :])


def _s5(u_tm, nb, abar, bmat, cmat, vecs, wglu):
    rows_total = u_tm.shape[0]
    blk = S5_TB * nb
    row = pl.BlockSpec((blk, S5_WIDTH), lambda t: (t, 0))
    return pl.pallas_call(
        functools.partial(_s5_body, nb),
        grid=(rows_total // blk,),
        in_specs=[row, _const_spec(abar.shape), _const_spec(bmat.shape), _const_spec(cmat.shape),
                  _const_spec(vecs.shape), _const_spec(wglu.shape)],
        out_specs=row,
        out_shape=jax.ShapeDtypeStruct(u_tm.shape, F32),
        scratch_shapes=[pltpu.VMEM((blk, 2 * S5_NSTATE), F32), pltpu.VMEM((nb, 2 * S5_NSTATE), F32)],
        compiler_params=pltpu.CompilerParams(dimension_semantics=("arbitrary",),
                                             vmem_limit_bytes=VMEM_LIMIT),
        name="s5",
    )(u_tm, abar, bmat, cmat, vecs, wglu)


def _gdn_body(qkv_ref, gate_ref, small_ref, conv_ref, vec_ref, o_ref, carry_ref, s_ref):
    tb = MIX_TB
    width = GDN_WIDTH

    @pl.when(pl.program_id(1) == 0)
    def _():
        carry_ref[...] = jnp.zeros(carry_ref.shape, F32)
        s_ref[...] = jnp.zeros(s_ref.shape, F32)

    x = qkv_ref[...]
    ext = jnp.concatenate([carry_ref[...], x], axis=0)
    carry_ref[...] = x[tb - 8:tb, :]
    conv = x * conv_ref[GDN_CONV - 1:GDN_CONV, :]
    for s in range(1, GDN_CONV):
        conv = conv + pltpu.roll(ext, s, axis=0)[8:8 + tb] * conv_ref[GDN_CONV - 1 - s:GDN_CONV - s, :]
    act = conv * _sigmoid(conv)
    q = act[:, 0:width]
    k = act[:, width:2 * width]
    v = act[:, 2 * width:3 * width]

    head_ones = jnp.where(_block_ones(width, GDN_HEAD), 1.0, 0.0).astype(BF16)
    l2 = lambda z: z * lax.rsqrt(jnp.maximum(_dot_exact_lhs(z * z, head_ones, 2), L2_EPS * L2_EPS))
    q = l2(q) * (GDN_HEAD ** -0.5)
    k = l2(k)

    expand = jnp.where(_iota((P_SMALL, 2 * width), 0) == _iota((P_SMALL, 2 * width), 1) // GDN_HEAD,
                       1.0, 0.0).astype(BF16)
    logits = _dot_exact_lhs(small_ref[...], expand, 3)
    beta = _sigmoid(logits[:, 0:width])
    g = -jnp.exp(vec_ref[0:1, :]) * _softplus(logits[:, width:2 * width] + vec_ref[1:2, :])

    lower, ones = _chunk_masks(tb)
    upper = (_iota((tb, width), 0) % CHUNK > _iota((tb, width), 1) % GDN_HEAD) & (
        _iota((tb, width), 1) % GDN_HEAD < CHUNK)
    gam = _dot_exact_rhs(lower, g, 3)
    gam_end = _dot_exact_rhs(ones, g, 3)
    dlog = _dot_exact_rhs(lower, jnp.where(upper, g, 0.0), 3)
    e_gam = jnp.exp(gam)
    k_beta = k * beta
    v_beta = v * beta
    w_rhs = k_beta * e_gam
    q_dec = q * e_gam
    k_dec = k * jnp.exp(gam_end - gam)
    c_dec = jnp.exp(gam_end)

    ri = _iota((CHUNK, CHUNK), 0)
    ci = _iota((CHUNK, CHUNK), 1)
    strict = ci < ri
    incl = ci <= ri

    outs = []
    for c in range(tb // CHUNK):
        sl = slice(c * CHUNK, (c + 1) * CHUNK)
        ns, attns = [], []
        for h in range(GDN_HEADS):
            hl = slice(h * GDN_HEAD, (h + 1) * GDN_HEAD)
            p1 = _bdot_nt(jnp.concatenate([k_beta[sl, hl], q[sl, hl]], axis=0), k[sl, hl])
            dec = jnp.exp(dlog[sl, h * GDN_HEAD:h * GDN_HEAD + CHUNK])
            ns.append(jnp.where(strict, -(p1[0:CHUNK] * dec), 0.0))
            attns.append(jnp.where(incl, p1[CHUNK:2 * CHUNK] * dec, 0.0))
        tinv = _unit_lower_inverse(ns)
        o_heads = []
        for h in range(GDN_HEADS):
            hl = slice(h * GDN_HEAD, (h + 1) * GDN_HEAD)
            sol = _bdot(tinv[h], jnp.concatenate([v_beta[sl, hl], w_rhs[sl, hl]], axis=1))
            u = sol[:, 0:GDN_HEAD]
            w = sol[:, GDN_HEAD:2 * GDN_HEAD]
            st = s_ref[h]
            wq = _bdot(jnp.concatenate([w, q_dec[sl, hl]], axis=0), st)
            v_new = u - wq[0:CHUNK]
            o_heads.append(wq[CHUNK:2 * CHUNK] + _bdot(attns[h], v_new))
            s_ref[h] = st * c_dec[c * CHUNK:c * CHUNK + 1, hl] + _bdot_tn(k_dec[sl, hl], v_new)
        outs.append(jnp.concatenate(o_heads, axis=1))

    o = jnp.concatenate(outs, axis=0)
    ms = _dot_exact_lhs(o * o, head_ones, 2) * (1.0 / GDN_HEAD)
    o = o * lax.rsqrt(ms + RMS_EPS) * vec_ref[2:3, :]
    gate = gate_ref[...]
    o_ref[...] = o * (gate * _sigmoid(gate))


def _gdn(qkv, gate, small, conv_w, vecs):
    bsz, t_len, _ = qkv.shape
    row = lambda w: pl.BlockSpec((None, MIX_TB, w), lambda b, t: (b, t, 0))
    return pl.pallas_call(
        _gdn_body,
        grid=(bsz, t_len // MIX_TB),
        in_specs=[row(_QKV_W), row(GDN_WIDTH), row(P_SMALL), _const_spec(conv_w.shape), _const_spec(vecs.shape)],
        out_specs=row(GDN_WIDTH),
        out_shape=jax.ShapeDtypeStruct((bsz, t_len, GDN_WIDTH), F32),
        scratch_shapes=[pltpu.VMEM((8, _QKV_W), F32), pltpu.VMEM((GDN_HEADS, GDN_HEAD, GDN_HEAD), F32)],
        compiler_params=pltpu.CompilerParams(dimension_semantics=("parallel", "arbitrary"),
                                             vmem_limit_bytes=VMEM_LIMIT),
        name="gdn",
    )(qkv, gate, small, conv_w, vecs)


def _pad_rows(w, rows, at):
    out = jnp.zeros((rows, w.shape[1]), w.dtype)
    return lax.dynamic_update_slice(out, w, (at, 0))


def _proj_weight(w_in_l, w_vres_l):
    o_s5 = RW_COLS
    o_qkv = o_s5 + S5_WIDTH
    o_gate = o_qkv + _QKV_W
    o_ba = o_gate + GDN_WIDTH
    small = [w_in_l[:, o_ba:o_ba + 2 * GDN_HEADS]]
    used = 2 * GDN_HEADS
    if w_vres_l is not None:
        small.append(w_vres_l)
        used += RW_V_RANK
    small.append(jnp.zeros((D_MODEL, P_SMALL - used), w_in_l.dtype))
    return jnp.concatenate([w_in_l[:, 0:RW_COLS], w_in_l[:, o_qkv:o_gate], w_in_l[:, o_gate:o_ba],
                            w_in_l[:, o_s5:o_qkv]] + small, axis=1).astype(BF16)


def _s5_matrices(abar_re, abar_im, bb_re, bb_im, c_re, c_im):
    g, n, hc = S5_GROUPS, S5_STATE, S5_GROUP_DIM
    eye = jnp.eye(g, dtype=F32)
    place_b = lambda bb: jnp.einsum("gnh,gk->ghkn", bb.reshape(g, n, hc), eye).reshape(g * hc, g * n)
    bmat = jnp.concatenate([place_b(bb_re), place_b(bb_im)], axis=1)
    place_c = lambda cc: jnp.einsum("ghn,gk->gnkh", cc, eye).reshape(g * n, g * hc)
    cmat = jnp.concatenate([place_c(c_re), -place_c(c_im)], axis=0)
    abar = jnp.concatenate([abar_re.reshape(1, g * n), abar_im.reshape(1, g * n)], axis=0)
    return abar, bmat.astype(BF16), cmat.astype(BF16)


def kernel(x, norm_gain, ffn_w1, ffn_w3, ffn_w2, w_in, w_in_vres, w_out, rwkv_mu, rwkv_w0, rwkv_w_w2, rwkv_a0, rwkv_w_a2, rwkv_w_g2, rwkv_k_k, rwkv_k_a, rwkv_r_k, rwkv_ln_w, rwkv_ln_b, rwkv_v0, rwkv_w_v2, s5_a_re, s5_a_im, s5_log_dt, s5_b_re, s5_b_im, s5_c_re, s5_c_im, s5_d, s5_w_glu, s5_b_glu, gdn_conv_w, gdn_a_log, gdn_dt_bias, gdn_norm_w):
    bsz, t_len, _ = x.shape
    depth = norm_gain.shape[0]
    w1 = ffn_w1.astype(BF16)
    w3 = ffn_w3.astype(BF16)
    w2 = ffn_w2.astype(BF16)
    x = x.astype(F32)
    v_first = None
    for l in range(depth):
        x = _ffn(x, norm_gain[l, 0:2], w1[l, 0], w3[l, 0], w2[l, 0])
        w_proj = _proj_weight(w_in[l], None if l == 0 else w_in_vres[l - 1])
        p_rw, p_qkv, p_gate, p_s5, p_small = _inproj(x, norm_gain[l, 2:3], w_proj)

        zero = jnp.zeros((RW_WIDTH,), F32)
        vecs = jnp.stack([
            rwkv_w0[l], rwkv_a0[l], rwkv_k_k[l], rwkv_k_a[l], rwkv_r_k[l].reshape(RW_WIDTH),
            rwkv_ln_w[l], rwkv_ln_b[l], zero, rwkv_v0[l - 1] if l > 0 else zero, zero,
            rwkv_mu[l, 0:256], rwkv_mu[l, 256:512], rwkv_mu[l, 512:768], rwkv_mu[l, 768:1024],
            zero, zero]).astype(F32)
        wlr = jnp.zeros((128, 512), F32)
        wlr = wlr.at[0:64, 0:256].set(rwkv_w_w2[l]).at[64:128, 256:512].set(rwkv_w_a2[l]).astype(BF16)
        wg2 = rwkv_w_g2[l].astype(BF16)
        if l == 0:
            y_a, v_first = _rwkv(p_rw, None, None, vecs, wlr, wg2, None)
        else:
            wv2 = _pad_rows(rwkv_w_v2[l - 1], P_SMALL, 2 * GDN_HEADS).astype(BF16)
            y_a = _rwkv(p_rw, p_small, v_first, vecs, wlr, wg2, wv2)

        abar_re, abar_im, bb_re, bb_im = _s5_discretize(s5_a_re[l], s5_a_im[l], s5_log_dt[l], s5_b_re[l], s5_b_im[l])
        abar, bmat, cmat = _s5_matrices(abar_re, abar_im, bb_re, bb_im, s5_c_re[l], s5_c_im[l])
        s5_vecs = jnp.zeros((8, S5_WIDTH), F32).at[0].set(s5_d[l]).at[1].set(s5_b_glu[l])
        y_b = _s5(p_s5.reshape(t_len * bsz, S5_WIDTH), bsz, abar, bmat, cmat, s5_vecs,
                  s5_w_glu[l].astype(BF16)).reshape(t_len, bsz * S5_WIDTH)

        rep = lambda z: jnp.repeat(z.astype(F32), GDN_HEAD)
        gdn_vecs = jnp.zeros((8, GDN_WIDTH), F32)
        gdn_vecs = gdn_vecs.at[0].set(rep(gdn_a_log[l])).at[1].set(rep(gdn_dt_bias[l]))
        gdn_vecs = gdn_vecs.at[2].set(jnp.tile(gdn_norm_w[l], GDN_HEADS))
        y_c = _gdn(p_qkv, p_gate, p_small, gdn_conv_w[l].astype(F32), gdn_vecs)

        gains = jnp.stack([norm_gain[l, 3], norm_gain[l, 4], norm_gain[l, 5]])
        x = _mixout_ffn(x, y_a, y_b, y_c, gains, w_out[l].astype(BF16), w1[l, 1], w3[l, 1], w2[l, 1])
    return x
```

```python
import functools
import math

import jax
import jax.numpy as jnp
from jax import lax
from jax.experimental import pallas as pl
from jax.experimental.pallas import tpu as pltpu

F32 = jnp.float32
BF16 = jnp.bfloat16

D_MODEL = 1024
D_FF = 2816
RMS_EPS = 1e-6
L2_EPS = 1e-12

RW_WIDTH = 256
RW_HEAD = 64
RW_HEADS = 4
RW_GN_EPS = 64e-5
RW_COLS = 1024
RW_V_RANK = 32

S5_WIDTH = 256
S5_GROUPS = 16
S5_GROUP_DIM = 16
S5_STATE = 64
S5_NSTATE = S5_GROUPS * S5_STATE

GDN_WIDTH = 512
GDN_HEAD = 128
GDN_HEADS = 4
GDN_CONV = 4

P_SMALL = 128

CHUNK = 64
MIX_TB = 256
MIX_NB = 2
S5_TB = 128
FFN_TM = 512
FFN_TF = 256
VMEM_LIMIT = 56 * 1024 * 1024


def _bdot(a, b):
    return jnp.dot(a.astype(BF16), b.astype(BF16), preferred_element_type=F32)


def _bdot_tn(a, b):
    return lax.dot_general(a.astype(BF16), b.astype(BF16), (((0,), (0,)), ((), ())),
                           preferred_element_type=F32)


def _split(x, n):
    pieces = []
    r = x
    for i in range(n):
        p = r.astype(BF16)
        pieces.append(p)
        if i + 1 < n:
            r = r - p.astype(F32)
    return pieces


def _dot_exact_rhs(w01, x, n=2):
    out = None
    for p in _split(x, n):
        t = jnp.dot(w01, p, preferred_element_type=F32)
        out = t if out is None else out + t
    return out


def _dot_exact_lhs(x, w01, n=2):
    out = None
    for p in _split(x, n):
        t = jnp.dot(p, w01, preferred_element_type=F32)
        out = t if out is None else out + t
    return out


def _rms(z, gain):
    ms = jnp.mean(z * z, axis=-1, keepdims=True)
    return z * lax.rsqrt(ms + RMS_EPS) * gain


def _sigmoid(x):
    return 1.0 / (1.0 + jnp.exp(-x))


def _softplus(x):
    return jnp.maximum(x, 0.0) + jnp.log1p(jnp.exp(-jnp.abs(x)))


def _iota(shape, dim):
    return lax.broadcasted_iota(jnp.int32, shape, dim)


def _chunk_lower(tb):
    r = _iota((tb, tb), 0)
    c = _iota((tb, tb), 1)
    return jnp.where(((r // CHUNK) == (c // CHUNK)) & (c <= r), 1.0, 0.0).astype(BF16)


def _chunk_last_rows(x):
    parts = []
    for c in range(x.shape[0] // CHUNK):
        last = x[(c + 1) * CHUNK - 1:(c + 1) * CHUNK, :]
        parts.append(jnp.broadcast_to(last, (CHUNK, x.shape[1])))
    return jnp.concatenate(parts, axis=0)


def _block_ones(n, blk):
    r = _iota((n, n), 0)
    c = _iota((n, n), 1)
    return (r // blk) == (c // blk)


def _head_block_diag(x, bdm):
    reps = bdm.shape[0] // x.shape[0]
    return jnp.where(bdm, jnp.concatenate([x] * reps, axis=0), 0.0).astype(BF16)


def _unit_lower_inverse_cat(ns, bdm):
    width = ns[0].shape[1]
    eye = jnp.where(_iota((CHUNK, width), 1) % CHUNK == _iota((CHUNK, width), 0), 1.0, 0.0).astype(F32)
    ts = [eye + n for n in ns]
    ps = [jnp.dot(n.astype(BF16), _head_block_diag(n, bdm), preferred_element_type=F32) for n in ns]
    levels = int(math.log2(CHUNK)) - 1
    for lvl in range(levels):
        new_ts, new_ps = [], []
        for t, p in zip(ts, ps):
            w = _head_block_diag(p, bdm)
            if lvl == levels - 1:
                new_ts.append(t + jnp.dot(t.astype(BF16), w, preferred_element_type=F32))
            else:
                both = jnp.dot(jnp.concatenate([t, p], axis=0).astype(BF16), w, preferred_element_type=F32)
                new_ts.append(t + both[0:CHUNK])
                new_ps.append(both[CHUNK:2 * CHUNK])
        ts, ps = new_ts, new_ps
    return ts


def _const_spec(shape):
    nd = len(shape)
    return pl.BlockSpec(shape, lambda *_: (0,) * nd, pipeline_mode=pl.Buffered(1))


_QKV_W = 3 * GDN_WIDTH
_PROJ_WIDTHS = (RW_COLS, _QKV_W, GDN_WIDTH, S5_WIDTH, P_SMALL)


def _ffn_core(x, gi, go, w1_ref, w3_ref, w2_ref):
    h = _rms(x, gi).astype(BF16)
    acc = jnp.zeros(x.shape, F32)
    for f in range(D_FF // FFN_TF):
        cs = slice(f * FFN_TF, (f + 1) * FFN_TF)
        a = jnp.dot(h, w1_ref[:, cs], preferred_element_type=F32)
        b = jnp.dot(h, w3_ref[:, cs], preferred_element_type=F32)
        act = (a * _sigmoid(a) * b).astype(BF16)
        acc = acc + jnp.dot(act, w2_ref[cs, :], preferred_element_type=F32)
    return x + 0.5 * _rms(acc, go)


def _ffn_inproj_body(x_ref, g_ref, w1_ref, w3_ref, w2_ref, wp_ref, o_ref, *proj_refs):
    x = _ffn_core(x_ref[...], g_ref[0:1, :], g_ref[1:2, :], w1_ref, w3_ref, w2_ref)
    o_ref[...] = x
    h = _rms(x, g_ref[2:3, :]).astype(BF16)
    lo = 0
    for p_ref in proj_refs:
        hi = lo + p_ref.shape[-1]
        p_ref[...] = jnp.dot(h, wp_ref[:, lo:hi], preferred_element_type=F32)
        lo = hi


def _mixout_ffn_body(x_ref, ya_ref, yb_ref, yc_ref, g_ref, wo_ref, w1_ref, w3_ref, w2_ref, o_ref):
    mixed = (jnp.dot(ya_ref[...].astype(BF16), wo_ref[0:256, :], preferred_element_type=F32)
             + jnp.dot(yb_ref[...].astype(BF16), wo_ref[256:512, :], preferred_element_type=F32)
             + jnp.dot(yc_ref[...].astype(BF16), wo_ref[512:1024, :], preferred_element_type=F32))
    x = x_ref[...] + _rms(mixed, g_ref[0:1, :])
    o_ref[...] = _ffn_core(x, g_ref[1:2, :], g_ref[2:3, :], w1_ref, w3_ref, w2_ref)


def _row_spec(width):
    return pl.BlockSpec((None, FFN_TM, width), lambda b, t: (b, t, 0))


def _ffn_inproj(x, gains, w1, w3, w2, wp):
    bsz, t_len, _ = x.shape
    sds = lambda wd: jax.ShapeDtypeStruct((bsz, t_len, wd), F32)
    return pl.pallas_call(
        _ffn_inproj_body,
        grid=(bsz, t_len // FFN_TM),
        in_specs=[_row_spec(D_MODEL)] + [_const_spec(a.shape) for a in (gains, w1, w3, w2, wp)],
        out_specs=[_row_spec(D_MODEL)] + [_row_spec(wd) for wd in _PROJ_WIDTHS],
        out_shape=[sds(D_MODEL)] + [sds(wd) for wd in _PROJ_WIDTHS],
        compiler_params=pltpu.CompilerParams(dimension_semantics=("parallel", "parallel"),
                                             vmem_limit_bytes=VMEM_LIMIT),
        name="ffn_inproj",
    )(x, gains, w1, w3, w2, wp)


def _mixout_ffn(x, ya, yb, yc, gains, wo, w1, w3, w2):
    bsz, t_len, _ = x.shape
    return pl.pallas_call(
        _mixout_ffn_body,
        grid=(bsz, t_len // FFN_TM),
        in_specs=[_row_spec(D_MODEL), _row_spec(RW_WIDTH), _row_spec(S5_WIDTH), _row_spec(GDN_WIDTH)]
        + [_const_spec(a.shape) for a in (gains, wo, w1, w3, w2)],
        out_specs=_row_spec(D_MODEL),
        out_shape=jax.ShapeDtypeStruct(x.shape, F32),
        compiler_params=pltpu.CompilerParams(dimension_semantics=("parallel", "parallel"),
                                             vmem_limit_bytes=VMEM_LIMIT),
        name="mixout_ffn",
    )(x, ya, yb, yc, gains, wo, w1, w3, w2)


def _mix_spec(width):
    return pl.BlockSpec((MIX_NB, MIX_TB, width), lambda b, t: (b, t, 0))


def _rwkv_body(has_vres, *refs):
    if has_vres:
        (p_ref, small_ref, vfirst_ref, vec_ref, wlr_ref, wg2_ref, wv2_ref,
         y_ref, prev_ref, h_ref) = refs
    else:
        (p_ref, vec_ref, wlr_ref, wg2_ref, y_ref, vout_ref, prev_ref, h_ref) = refs
    tb = MIX_TB
    width = RW_WIDTH
    n_chunks = tb // CHUNK

    @pl.when(pl.program_id(1) == 0)
    def _():
        prev_ref[...] = jnp.zeros(prev_ref.shape, F32)
        h_ref[...] = jnp.zeros(h_ref.shape, F32)

    vec = lambda i: vec_ref[i:i + 1, :]
    mu = jnp.concatenate([vec(10), vec(11), vec(12), vec(13)], axis=1)
    head_ones = jnp.where(_block_ones(width, RW_HEAD), 1.0, 0.0).astype(BF16)
    lower = _chunk_lower(tb)
    first = _iota((tb, 1), 0) == 0
    ri = _iota((CHUNK, width), 0)
    ci = _iota((CHUNK, width), 1) % CHUNK
    strict = ci < ri
    incl = ci <= ri
    bdm = _block_ones(width, RW_HEAD)
    bd = lambda x: _head_block_diag(x, bdm)

    seqs = []
    for s in range(MIX_NB):
        cols = p_ref[s]
        shifted = jnp.where(first, prev_ref[s, 0:1, :], pltpu.roll(cols, 1, axis=0))
        prev_ref[s, 0:1, :] = cols[tb - 1:tb, :]
        z = cols + mu * (shifted - cols)
        r = z[:, 0:256]
        k = z[:, 256:512]
        v = z[:, 512:768]
        lr_in = z[:, 768:896]
        lr_act = jnp.where(_iota((1, 128), 1) < 64, jnp.tanh(lr_in), lr_in)
        lr = _bdot(lr_act, wlr_ref[...])
        lw = -jnp.exp(-_softplus(-(vec(0) + lr[:, 0:256])) - 0.5)
        a = _sigmoid(vec(1) + lr[:, 256:512])
        g = _bdot(_sigmoid(z[:, 896:1024]), wg2_ref[...])
        if has_vres:
            mix = _sigmoid(vec(8) + _bdot(small_ref[s], wv2_ref[...]))
            v = v + (vfirst_ref[s] - v) * mix
        else:
            vout_ref[s] = v
        kkr = k * vec(2)
        kk = kkr * lax.rsqrt(jnp.maximum(_dot_exact_lhs(kkr * kkr, head_ones), L2_EPS * L2_EPS))
        k2 = k * (1.0 + (a - 1.0) * vec(3))
        bv = kk * a
        cl = _dot_exact_rhs(lower, lw)
        cl_end = _chunk_last_rows(cl)
        e_inv = jnp.exp(-cl)
        e_end = jnp.exp(cl_end - cl)
        seqs.append(dict(
            r=r, v=v, g=g, k2=k2,
            rt=r * jnp.exp(cl), at=-kk * jnp.exp(cl - lw), bt=bv * e_inv, kt=k2 * e_inv,
            bh=bv * e_end, kh=k2 * e_end, g_end=jnp.exp(cl_end)))

    items = [(s, c) for s in range(MIX_NB) for c in range(n_chunks)]
    pb, pk = {}, {}
    for s, c in items:
        q = seqs[s]
        sl = slice(c * CHUNK, (c + 1) * CHUNK)
        lhs = jnp.concatenate([q["at"][sl], q["rt"][sl]], axis=0).astype(BF16)
        pb[s, c] = lax.dot_general(lhs, bd(q["bt"][sl]), (((1,), (1,)), ((), ())), preferred_element_type=F32)
        pk[s, c] = lax.dot_general(lhs, bd(q["kt"][sl]), (((1,), (1,)), ((), ())), preferred_element_type=F32)
    tinv = dict(zip(items, _unit_lower_inverse_cat(
        [jnp.where(strict, pb[key][0:CHUNK], 0.0) for key in items], bdm)))
    chunk = {}
    for s, c in items:
        q = seqs[s]
        sl = slice(c * CHUNK, (c + 1) * CHUNK)
        v_c, rt_c = q["v"][sl], q["rt"][sl]
        a_k = jnp.concatenate([jnp.where(strict, pk[s, c][0:CHUNK], 0.0),
                               jnp.where(incl, pk[s, c][CHUNK:2 * CHUNK], 0.0)], axis=0).astype(BF16)
        a_rb = jnp.where(incl, pb[s, c][CHUNK:2 * CHUNK], 0.0).astype(BF16)
        akv = jnp.dot(a_k, bd(v_c), preferred_element_type=F32)
        wu = jnp.dot(tinv[s, c].astype(BF16), jnp.concatenate([bd(q["at"][sl]), bd(akv[0:CHUNK])], axis=1),
                     preferred_element_type=F32)
        wt, ut = wu[:, 0:width], wu[:, width:2 * width]
        rwu = jnp.dot(a_rb, jnp.concatenate([bd(wt), bd(ut)], axis=1), preferred_element_type=F32)
        gcol = jnp.transpose(jnp.concatenate([q["g_end"][sl]] * (width // CHUNK), axis=0))
        m = jnp.where(bdm, _bdot_tn(q["bh"][sl], wt), 0.0)
        chunk[s, c] = dict(
            m_ry=jnp.concatenate([m, rt_c + rwu[:, 0:width]], axis=0).astype(BF16),
            y0=rwu[:, width:2 * width] + akv[CHUNK:2 * CHUNK],
            n0=jnp.where(bdm, _bdot_tn(jnp.concatenate([q["bh"][sl], q["kh"][sl]], axis=0),
                                       jnp.concatenate([ut, v_c], axis=0)), 0.0),
            gcol=gcol)

    states = [h_ref[s] for s in range(MIX_NB)]
    ys = [[] for _ in range(MIX_NB)]
    for c in range(n_chunks):
        for s in range(MIX_NB):
            d = chunk[s, c]
            prod = jnp.dot(d["m_ry"], states[s].astype(BF16), preferred_element_type=F32)
            ys[s].append(prod[width:width + CHUNK] + d["y0"])
            states[s] = d["gcol"] * states[s] + prod[0:width] + d["n0"]
    for s in range(MIX_NB):
        h_ref[s] = states[s]

    inv_n = 1.0 / RW_HEAD
    for s in range(MIX_NB):
        q = seqs[s]
        y = jnp.concatenate(ys[s], axis=0)
        yc = y - _dot_exact_lhs(y, head_ones) * inv_n
        var = _dot_exact_lhs(yc * yc, head_ones) * inv_n
        yn = yc * lax.rsqrt(var + RW_GN_EPS) * vec(5) + vec(6)
        yn = yn + _dot_exact_lhs(q["r"] * q["k2"] * vec(4), head_ones) * q["v"]
        y_ref[s] = yn * q["g"]


def _rwkv(p_rw, small, v_first, vecs, wlr, wg2, wv2):
    bsz, t_len, _ = p_rw.shape
    has_vres = v_first is not None
    y_sds = jax.ShapeDtypeStruct((bsz, t_len, RW_WIDTH), F32)
    if has_vres:
        args = (p_rw, small, v_first, vecs, wlr, wg2, wv2)
        in_specs = [_mix_spec(RW_COLS), _mix_spec(P_SMALL), _mix_spec(RW_WIDTH)] + [
            _const_spec(a.shape) for a in (vecs, wlr, wg2, wv2)]
        out_specs, out_shape = _mix_spec(RW_WIDTH), y_sds
    else:
        args = (p_rw, vecs, wlr, wg2)
        in_specs = [_mix_spec(RW_COLS)] + [_const_spec(a.shape) for a in (vecs, wlr, wg2)]
        out_specs, out_shape = [_mix_spec(RW_WIDTH), _mix_spec(RW_WIDTH)], [y_sds, y_sds]
    return pl.pallas_call(
        functools.partial(_rwkv_body, has_vres),
        grid=(bsz // MIX_NB, t_len // MIX_TB),
        in_specs=in_specs,
        out_specs=out_specs,
        out_shape=out_shape,
        scratch_shapes=[pltpu.VMEM((MIX_NB, 8, RW_COLS), F32), pltpu.VMEM((MIX_NB, RW_WIDTH, RW_WIDTH), F32)],
        compiler_params=pltpu.CompilerParams(dimension_semantics=("parallel", "arbitrary"),
                                             vmem_limit_bytes=VMEM_LIMIT),
        name="rwkv_vres" if has_vres else "rwkv",
    )(*args)


def _s5_disc_body(are_ref, aim_ref, ldt_ref, bre_ref, bim_ref, abar_re_ref, abar_im_ref, bbre_ref, bbim_ref):
    a_re = are_ref[...]
    a_im = aim_ref[...]
    dt = jnp.exp(ldt_ref[...])
    mag = jnp.exp(dt * a_re)
    ang = dt * a_im
    abar_re = mag * jnp.cos(ang)
    abar_im = mag * jnp.sin(ang)
    den = a_re * a_re + a_im * a_im
    num_re = abar_re - 1.0
    coef_re = (num_re * a_re + abar_im * a_im) / den
    coef_im = (abar_im * a_re - num_re * a_im) / den
    abar_re_ref[...] = abar_re
    abar_im_ref[...] = abar_im
    b_re = bre_ref[...]
    b_im = bim_ref[...]
    bbre_ref[...] = coef_re * b_re - coef_im * b_im
    bbim_ref[...] = coef_re * b_im + coef_im * b_re


def _s5_discretize(a_re, a_im, log_dt, b_re, b_im):
    n = S5_NSTATE
    col = lambda z: z.astype(F32).reshape(n, 1)
    ldt = jnp.broadcast_to(log_dt.astype(F32)[:, None], (S5_GROUPS, S5_STATE))
    mat = lambda z: z.reshape(n, S5_GROUP_DIM)
    return pl.pallas_call(
        _s5_disc_body,
        out_shape=[jax.ShapeDtypeStruct((n, 1), F32)] * 2 + [jax.ShapeDtypeStruct((n, S5_GROUP_DIM), F32)] * 2,
        name="s5_discretize",
    )(col(a_re), col(a_im), col(ldt), mat(b_re), mat(b_im))


def _gelu_tanh(x):
    return 0.5 * x * (1.0 + jnp.tanh(math.sqrt(2.0 / math.pi) * (x + 0.044715 * (x * x * x))))


def _s5_body(u_ref, abar_ref, bmat_ref, cmat_ref, vec_ref, wglu_ref, y_ref, x_ref, tm_ref, st_ref):
    n = S5_NSTATE
    nb = u_ref.shape[0]
    halves = S5_WIDTH // 128

    @pl.when(pl.program_id(0) == 0)
    def _():
        st_ref[...] = jnp.zeros(st_ref.shape, F32)

    for b in range(nb):
        ub = u_ref[b]
        for j in range(halves):
            tm_ref[j, pl.ds(b, S5_TB, stride=nb), :] = ub[:, j * 128:(j + 1) * 128]
    u = jnp.concatenate([tm_ref[j] for j in range(halves)], axis=1)
    x_ref[...] = _bdot(u, bmat_ref[...])
    a_re = jnp.broadcast_to(abar_ref[0:1, :], (nb, n))
    a_im = jnp.broadcast_to(abar_ref[1:2, :], (nb, n))

    def step(t, carry):
        s_re, s_im = carry
        rows = pl.ds(pl.multiple_of(t * nb, nb), nb)
        n_re = a_re * s_re - a_im * s_im + x_ref[rows, 0:n]
        n_im = a_re * s_im + a_im * s_re + x_ref[rows, n:2 * n]
        x_ref[rows, 0:n] = n_re
        x_ref[rows, n:2 * n] = n_im
        return n_re, n_im

    s_re, s_im = lax.fori_loop(0, S5_TB, step, (st_ref[:, 0:n], st_ref[:, n:2 * n]))
    st_ref[:, 0:n] = s_re
    st_ref[:, n:2 * n] = s_im

    y = _bdot(x_ref[...], cmat_ref[...]) + vec_ref[0:1, :] * u
    zz = _gelu_tanh(y)
    res = zz * _sigmoid(_bdot(zz, wglu_ref[...]) + vec_ref[1:2, :])
    for j in range(halves):
        tm_ref[j] = res[:, j * 128:(j + 1) * 128]
    for b in range(nb):
        y_ref[b] = jnp.concatenate([tm_ref[j, pl.ds(b, S5_TB, stride=nb), :] for j in range(halves)], axis=1)


def _s5(u, abar, bmat, cmat, vecs, wglu):
    bsz, t_len, _ = u.shape
    blk = pl.BlockSpec((bsz, S5_TB, S5_WIDTH), lambda t: (0, t, 0))
    return pl.pallas_call(
        _s5_body,
        grid=(t_len // S5_TB,),
        in_specs=[blk] + [_const_spec(a.shape) for a in (abar, bmat, cmat, vecs, wglu)],
        out_specs=blk,
        out_shape=jax.ShapeDtypeStruct(u.shape, F32),
        scratch_shapes=[pltpu.VMEM((S5_TB * bsz, 2 * S5_NSTATE), F32),
                        pltpu.VMEM((S5_WIDTH // 128, S5_TB * bsz, 128), F32),
                        pltpu.VMEM((bsz, 2 * S5_NSTATE), F32)],
        compiler_params=pltpu.CompilerParams(dimension_semantics=("arbitrary",),
                                             vmem_limit_bytes=VMEM_LIMIT),
        name="s5",
    )(u, abar, bmat, cmat, vecs, wglu)


def _gdn_body(qkv_ref, gate_ref, small_ref, conv_ref, vec_ref, o_ref, ext_ref, s_ref):
    tb = MIX_TB
    width = GDN_WIDTH
    n_chunks = tb // CHUNK
    cat = GDN_HEADS * CHUNK

    @pl.when(pl.program_id(1) == 0)
    def _():
        ext_ref[:, 0:8, :] = jnp.zeros((MIX_NB, 8, _QKV_W), F32)
        s_ref[...] = jnp.zeros(s_ref.shape, F32)

    def head_sumsq(z):
        parts = []
        for h in range(GDN_HEADS):
            zh = z[:, h * GDN_HEAD:(h + 1) * GDN_HEAD]
            parts.append(jnp.broadcast_to(jnp.sum(zh * zh, axis=-1, keepdims=True), zh.shape))
        return jnp.concatenate(parts, axis=1)

    l2 = lambda z: z * lax.rsqrt(jnp.maximum(head_sumsq(z), L2_EPS * L2_EPS))
    e_col = _iota((P_SMALL, 2 * width + cat), 1)
    e_src = jnp.where(e_col < 2 * width, e_col // GDN_HEAD, GDN_HEADS + (e_col - 2 * width) // CHUNK)
    expand = jnp.where(_iota((P_SMALL, 2 * width + cat), 0) == e_src, 1.0, 0.0).astype(BF16)
    lower = _chunk_lower(tb)
    upper = (_iota((tb, cat), 0) % CHUNK) > (_iota((tb, cat), 1) % CHUNK)
    ri = _iota((CHUNK, cat), 0)
    ci = _iota((CHUNK, cat), 1) % CHUNK
    strict = ci < ri
    incl = ci <= ri
    bdm = _block_ones(cat, CHUNK)
    bd_k = (_iota((cat, width), 0) // CHUNK) == (_iota((cat, width), 1) // GDN_HEAD)
    bd_sol = (_iota((cat, 2 * width), 0) // CHUNK) == (_iota((cat, 2 * width), 1) // (2 * GDN_HEAD))
    tile = lambda x: jnp.concatenate([x] * GDN_HEADS, axis=0)

    seqs = []
    for s in range(MIX_NB):
        x = qkv_ref[s]
        ext_ref[s, 8:8 + tb, :] = x
        conv = x * conv_ref[GDN_CONV - 1:GDN_CONV, :]
        for j in range(1, GDN_CONV):
            conv = conv + ext_ref[s, 8 - j:8 - j + tb, :] * conv_ref[GDN_CONV - 1 - j:GDN_CONV - j, :]
        ext_ref[s, 0:8, :] = x[tb - 8:tb, :]
        act = conv * _sigmoid(conv)
        q = l2(act[:, 0:width]) * (GDN_HEAD ** -0.5)
        k = l2(act[:, width:2 * width])
        v = act[:, 2 * width:3 * width]
        logits = _dot_exact_lhs(small_ref[s], expand)
        beta = _sigmoid(logits[:, 0:width])
        g = -jnp.exp(vec_ref[0:1, :]) * _softplus(logits[:, width:2 * width] + vec_ref[1:2, :])
        g_cat = -jnp.exp(vec_ref[3:4, 0:cat]) * _softplus(logits[:, 2 * width:2 * width + cat] + vec_ref[4:5, 0:cat])
        gam = _dot_exact_rhs(lower, g)
        gam_end = _chunk_last_rows(gam)
        e_gam = jnp.exp(gam)
        k_beta = k * beta
        seqs.append(dict(
            q=q.astype(BF16), k=k.astype(BF16), k_beta=k_beta.astype(BF16),
            dec=jnp.exp(_dot_exact_rhs(lower, jnp.where(upper, g_cat, 0.0))),
            v_beta=(v * beta).astype(BF16), w_rhs=(k_beta * e_gam).astype(BF16),
            q_dec=q * e_gam, k_dec=(k * jnp.exp(gam_end - gam)).astype(BF16), c_dec=jnp.exp(gam_end)))

    items = [(s, c) for s in range(MIX_NB) for c in range(n_chunks)]
    ns, attn = {}, {}
    for s, c in items:
        d = seqs[s]
        sl = slice(c * CHUNK, (c + 1) * CHUNK)
        k_bd = jnp.where(bd_k, tile(d["k"][sl]), 0.0).astype(BF16)
        p1 = lax.dot_general(jnp.concatenate([d["k_beta"][sl], d["q"][sl]], axis=0), k_bd,
                             (((1,), (1,)), ((), ())), preferred_element_type=F32)
        dec = d["dec"][sl]
        ns[s, c] = jnp.where(strict, -(p1[0:CHUNK] * dec), 0.0)
        attn[s, c] = jnp.where(incl, p1[CHUNK:2 * CHUNK] * dec, 0.0).astype(BF16)
    tinv = dict(zip(items, _unit_lower_inverse_cat([ns[key] for key in items], bdm)))
    chunk = {}
    for s, c in items:
        d = seqs[s]
        sl = slice(c * CHUNK, (c + 1) * CHUNK)
        rhs = jnp.concatenate([z[sl, h * GDN_HEAD:(h + 1) * GDN_HEAD] for h in range(GDN_HEADS)
                               for z in (d["v_beta"], d["w_rhs"])], axis=1)
        sol_all = jnp.dot(tinv[s, c].astype(BF16), jnp.where(bd_sol, tile(rhs), 0.0).astype(BF16),
                          preferred_element_type=F32).astype(BF16)
        qo_all = jnp.dot(attn[s, c], jnp.where(bd_sol, tile(sol_all), 0.0).astype(BF16),
                         preferred_element_type=F32)
        for h in range(GDN_HEADS):
            hl = slice(h * GDN_HEAD, (h + 1) * GDN_HEAD)
            uw = slice(2 * h * GDN_HEAD, 2 * (h + 1) * GDN_HEAD)
            sol = sol_all[:, uw]
            qo = qo_all[:, uw]
            mn = lax.dot_general(d["k_dec"][sl, hl], sol, (((0,), (0,)), ((), ())),
                                 preferred_element_type=F32)
            chunk[s, c, h] = dict(
                m_qd=jnp.concatenate([mn[:, GDN_HEAD:2 * GDN_HEAD],
                                      d["q_dec"][sl, hl] - qo[:, GDN_HEAD:2 * GDN_HEAD]], axis=0).astype(BF16),
                o0=qo[:, 0:GDN_HEAD], n0=mn[:, 0:GDN_HEAD], c_dec=d["c_dec"][c * CHUNK:c * CHUNK + 1, hl])

    states = {(s, h): s_ref[s, h] for s in range(MIX_NB) for h in range(GDN_HEADS)}
    outs = {}
    for c in range(n_chunks):
        for s in range(MIX_NB):
            for h in range(GDN_HEADS):
                d = chunk[s, c, h]
                st = states[s, h]
                prod = jnp.dot(d["m_qd"], st.astype(BF16), preferred_element_type=F32)
                outs[s, c, h] = prod[GDN_HEAD:GDN_HEAD + CHUNK] + d["o0"]
                states[s, h] = st * d["c_dec"] - prod[0:GDN_HEAD] + d["n0"]
    for s in range(MIX_NB):
        for h in range(GDN_HEADS):
            s_ref[s, h] = states[s, h]

    for s in range(MIX_NB):
        o = jnp.concatenate([jnp.concatenate([outs[s, c, h] for h in range(GDN_HEADS)], axis=1)
                             for c in range(n_chunks)], axis=0)
        gate = gate_ref[s]
        o_ref[s] = (o * lax.rsqrt(head_sumsq(o) * (1.0 / GDN_HEAD) + RMS_EPS) * vec_ref[2:3, :]
                    * (gate * _sigmoid(gate)))


def _gdn(qkv, gate, small, conv_w, vecs):
    bsz, t_len, _ = qkv.shape
    return pl.pallas_call(
        _gdn_body,
        grid=(bsz // MIX_NB, t_len // MIX_TB),
        in_specs=[_mix_spec(_QKV_W), _mix_spec(GDN_WIDTH), _mix_spec(P_SMALL),
                  _const_spec(conv_w.shape), _const_spec(vecs.shape)],
        out_specs=_mix_spec(GDN_WIDTH),
        out_shape=jax.ShapeDtypeStruct((bsz, t_len, GDN_WIDTH), F32),
        scratch_shapes=[pltpu.VMEM((MIX_NB, MIX_TB + 8, _QKV_W), F32),
                        pltpu.VMEM((MIX_NB, GDN_HEADS, GDN_HEAD, GDN_HEAD), F32)],
        compiler_params=pltpu.CompilerParams(dimension_semantics=("parallel", "arbitrary"),
                                             vmem_limit_bytes=VMEM_LIMIT),
        name="gdn",
    )(qkv, gate, small, conv_w, vecs)


def _pad_rows(w, rows, at):
    out = jnp.zeros((rows, w.shape[1]), w.dtype)
    return lax.dynamic_update_slice(out, w, (at, 0))


def _proj_weight(w_in_l, w_vres_l):
    o_s5 = RW_COLS
    o_qkv = o_s5 + S5_WIDTH
    o_gate = o_qkv + _QKV_W
    o_ba = o_gate + GDN_WIDTH
    small = [w_in_l[:, o_ba:o_ba + 2 * GDN_HEADS]]
    used = 2 * GDN_HEADS
    if w_vres_l is not None:
        small.append(w_vres_l)
        used += RW_V_RANK
    small.append(jnp.zeros((D_MODEL, P_SMALL - used), w_in_l.dtype))
    return jnp.concatenate([w_in_l[:, 0:RW_COLS], w_in_l[:, o_qkv:o_gate], w_in_l[:, o_gate:o_ba],
                            w_in_l[:, o_s5:o_qkv]] + small, axis=1).astype(BF16)


def _s5_matrices(abar_re, abar_im, bb_re, bb_im, c_re, c_im):
    g, n, hc = S5_GROUPS, S5_STATE, S5_GROUP_DIM
    eye = jnp.eye(g, dtype=F32)
    place_b = lambda bb: jnp.einsum("gnh,gk->ghkn", bb.reshape(g, n, hc), eye).reshape(g * hc, g * n)
    bmat = jnp.concatenate([place_b(bb_re), place_b(bb_im)], axis=1)
    place_c = lambda cc: jnp.einsum("ghn,gk->gnkh", cc, eye).reshape(g * n, g * hc)
    cmat = jnp.concatenate([place_c(c_re), -place_c(c_im)], axis=0)
    abar = jnp.concatenate([abar_re.reshape(1, g * n), abar_im.reshape(1, g * n)], axis=0)
    return abar, bmat.astype(BF16), cmat.astype(BF16)


def kernel(x, norm_gain, ffn_w1, ffn_w3, ffn_w2, w_in, w_in_vres, w_out, rwkv_mu, rwkv_w0, rwkv_w_w2, rwkv_a0, rwkv_w_a2, rwkv_w_g2, rwkv_k_k, rwkv_k_a, rwkv_r_k, rwkv_ln_w, rwkv_ln_b, rwkv_v0, rwkv_w_v2, s5_a_re, s5_a_im, s5_log_dt, s5_b_re, s5_b_im, s5_c_re, s5_c_im, s5_d, s5_w_glu, s5_b_glu, gdn_conv_w, gdn_a_log, gdn_dt_bias, gdn_norm_w):
    depth = norm_gain.shape[0]
    w1 = ffn_w1.astype(BF16)
    w3 = ffn_w3.astype(BF16)
    w2 = ffn_w2.astype(BF16)
    x = x.astype(F32)
    v_first = None
    for l in range(depth):
        w_proj = _proj_weight(w_in[l], None if l == 0 else w_in_vres[l - 1])
        x, p_rw, p_qkv, p_gate, p_s5, p_small = _ffn_inproj(
            x, norm_gain[l, 0:3], w1[l, 0], w3[l, 0], w2[l, 0], w_proj)

        zero = jnp.zeros((RW_WIDTH,), F32)
        vecs = jnp.stack([
            rwkv_w0[l], rwkv_a0[l], rwkv_k_k[l], rwkv_k_a[l], rwkv_r_k[l].reshape(RW_WIDTH),
            rwkv_ln_w[l], rwkv_ln_b[l], zero, rwkv_v0[l - 1] if l > 0 else zero, zero,
            rwkv_mu[l, 0:256], rwkv_mu[l, 256:512], rwkv_mu[l, 512:768], rwkv_mu[l, 768:1024],
            zero, zero]).astype(F32)
        wlr = jnp.zeros((128, 512), F32)
        wlr = wlr.at[0:64, 0:256].set(rwkv_w_w2[l]).at[64:128, 256:512].set(rwkv_w_a2[l]).astype(BF16)
        wg2 = rwkv_w_g2[l].astype(BF16)
        if l == 0:
            y_a, v_first = _rwkv(p_rw, None, None, vecs, wlr, wg2, None)
        else:
            wv2 = _pad_rows(rwkv_w_v2[l - 1], P_SMALL, 2 * GDN_HEADS).astype(BF16)
            y_a = _rwkv(p_rw, p_small, v_first, vecs, wlr, wg2, wv2)

        abar_re, abar_im, bb_re, bb_im = _s5_discretize(s5_a_re[l], s5_a_im[l], s5_log_dt[l], s5_b_re[l], s5_b_im[l])
        abar, bmat, cmat = _s5_matrices(abar_re, abar_im, bb_re, bb_im, s5_c_re[l], s5_c_im[l])
        s5_vecs = jnp.zeros((8, S5_WIDTH), F32).at[0].set(s5_d[l]).at[1].set(s5_b_glu[l])
        y_b = _s5(p_s5, abar, bmat, cmat, s5_vecs, s5_w_glu[l].astype(BF16))

        rep = lambda z: jnp.repeat(z.astype(F32), GDN_HEAD)
        rep_c = lambda z: jnp.pad(jnp.repeat(z.astype(F32), CHUNK), (0, GDN_WIDTH - GDN_HEADS * CHUNK))
        gdn_vecs = jnp.zeros((8, GDN_WIDTH), F32)
        gdn_vecs = gdn_vecs.at[0].set(rep(gdn_a_log[l])).at[1].set(rep(gdn_dt_bias[l]))
        gdn_vecs = gdn_vecs.at[2].set(jnp.tile(gdn_norm_w[l], GDN_HEADS))
        gdn_vecs = gdn_vecs.at[3].set(rep_c(gdn_a_log[l])).at[4].set(rep_c(gdn_dt_bias[l]))
        y_c = _gdn(p_qkv, p_gate, p_small, gdn_conv_w[l].astype(F32), gdn_vecs)

        x = _mixout_ffn(x, y_a, y_b, y_c, norm_gain[l, 3:6], w_out[l].astype(BF16), w1[l, 1], w3[l, 1], w2[l, 1])
    return x
```

```python
import functools
import math

import jax
import jax.numpy as jnp
from jax import lax
from jax.experimental import pallas as pl
from jax.experimental.pallas import tpu as pltpu

F32 = jnp.float32
BF16 = jnp.bfloat16

D_MODEL = 1024
D_FF = 2816
RMS_EPS = 1e-6
L2_EPS = 1e-12

RW_WIDTH = 256
RW_HEAD = 64
RW_HEADS = 4
RW_GN_EPS = 64e-5
RW_COLS = 1024
RW_V_RANK = 32

S5_WIDTH = 256
S5_GROUPS = 16
S5_GROUP_DIM = 16
S5_STATE = 64
S5_NSTATE = S5_GROUPS * S5_STATE

GDN_WIDTH = 512
GDN_HEAD = 128
GDN_HEADS = 4
GDN_CONV = 4

P_SMALL = 128

CHUNK = 64
MIX_TB = 256
MIX_NB = 2
S5_TB = 128
FFN_TM = 512
FFN_TF = 256
VMEM_LIMIT = 56 * 1024 * 1024


def _bdot(a, b):
    return jnp.dot(a.astype(BF16), b.astype(BF16), preferred_element_type=F32)


def _bdot_tn(a, b):
    return lax.dot_general(a.astype(BF16), b.astype(BF16), (((0,), (0,)), ((), ())),
                           preferred_element_type=F32)


def _split(x, n):
    pieces = []
    r = x
    for i in range(n):
        p = r.astype(BF16)
        pieces.append(p)
        if i + 1 < n:
            r = r - p.astype(F32)
    return pieces


def _dot_exact_rhs(w01, x, n=2):
    out = None
    for p in _split(x, n):
        t = jnp.dot(w01, p, preferred_element_type=F32)
        out = t if out is None else out + t
    return out


def _dot_exact_lhs(x, w01, n=2):
    out = None
    for p in _split(x, n):
        t = jnp.dot(p, w01, preferred_element_type=F32)
        out = t if out is None else out + t
    return out


def _rms(z, gain):
    ms = jnp.mean(z * z, axis=-1, keepdims=True)
    return z * lax.rsqrt(ms + RMS_EPS) * gain


def _sigmoid(x):
    return 1.0 / (1.0 + jnp.exp2(x * (-1.0 / math.log(2.0))))


def _softplus(x):
    return jnp.maximum(x, 0.0) + jnp.log1p(jnp.exp(-jnp.abs(x)))


def _iota(shape, dim):
    return lax.broadcasted_iota(jnp.int32, shape, dim)


def _chunk_lower(tb):
    r = _iota((tb, tb), 0)
    c = _iota((tb, tb), 1)
    return jnp.where(((r // CHUNK) == (c // CHUNK)) & (c <= r), 1.0, 0.0).astype(BF16)


def _chunk_last_rows(x):
    parts = []
    for c in range(x.shape[0] // CHUNK):
        last = x[(c + 1) * CHUNK - 1:(c + 1) * CHUNK, :]
        parts.append(jnp.broadcast_to(last, (CHUNK, x.shape[1])))
    return jnp.concatenate(parts, axis=0)


def _block_ones(n, blk):
    r = _iota((n, n), 0)
    c = _iota((n, n), 1)
    return (r // blk) == (c // blk)


def _head_block_diag(x, bdm):
    reps = bdm.shape[0] // x.shape[0]
    return jnp.where(bdm, jnp.concatenate([x] * reps, axis=0), 0.0).astype(BF16)


def _unit_lower_inverse_cat(ns, bdm):
    width = ns[0].shape[1]
    eye = jnp.where(_iota((CHUNK, width), 1) % CHUNK == _iota((CHUNK, width), 0), 1.0, 0.0).astype(F32)
    ts = [eye + n for n in ns]
    ps = [jnp.dot(n.astype(BF16), _head_block_diag(n, bdm), preferred_element_type=F32) for n in ns]
    levels = int(math.log2(CHUNK)) - 1
    for lvl in range(levels):
        new_ts, new_ps = [], []
        for t, p in zip(ts, ps):
            w = _head_block_diag(p, bdm)
            if lvl == levels - 1:
                new_ts.append(t + jnp.dot(t.astype(BF16), w, preferred_element_type=F32))
            else:
                both = jnp.dot(jnp.concatenate([t, p], axis=0).astype(BF16), w, preferred_element_type=F32)
                new_ts.append(t + both[0:CHUNK])
                new_ps.append(both[CHUNK:2 * CHUNK])
        ts, ps = new_ts, new_ps
    return ts


def _const_spec(shape):
    nd = len(shape)
    return pl.BlockSpec(shape, lambda *_: (0,) * nd, pipeline_mode=pl.Buffered(1))


_QKV_W = 3 * GDN_WIDTH
_PROJ_WIDTHS = (_QKV_W, RW_COLS, GDN_WIDTH, S5_WIDTH, P_SMALL)


def _head_sumsq(z, head):
    parts = []
    for h in range(z.shape[1] // head):
        zh = z[:, h * head:(h + 1) * head]
        parts.append(jnp.broadcast_to(jnp.sum(zh * zh, axis=-1, keepdims=True), zh.shape))
    return jnp.concatenate(parts, axis=1)


def _l2_heads(z, head):
    return z * lax.rsqrt(jnp.maximum(_head_sumsq(z, head), L2_EPS * L2_EPS))


def _ffn_core(x, gi, go, w1_ref, w3_ref, w2_ref):
    h = _rms(x, gi).astype(BF16)
    acc = jnp.zeros(x.shape, F32)
    for f in range(D_FF // FFN_TF):
        cs = slice(f * FFN_TF, (f + 1) * FFN_TF)
        a = jnp.dot(h, w1_ref[:, cs], preferred_element_type=F32)
        b = jnp.dot(h, w3_ref[:, cs], preferred_element_type=F32)
        act = (a * _sigmoid(a) * b).astype(BF16)
        acc = acc + jnp.dot(act, w2_ref[cs, :], preferred_element_type=F32)
    return x + 0.5 * _rms(acc, go)


def _ffn_inproj_body(x_ref, g_ref, w1_ref, w3_ref, w2_ref, wp_ref, pv_ref, o_ref,
                     qkv_ref, rw_ref, gate_ref, s5_ref, small_ref, cq_ref, cr_ref):
    tm = FFN_TM

    @pl.when(pl.program_id(1) == 0)
    def _():
        cq_ref[...] = jnp.zeros(cq_ref.shape, F32)
        cr_ref[...] = jnp.zeros(cr_ref.shape, F32)

    x = _ffn_core(x_ref[...], g_ref[0:1, :], g_ref[1:2, :], w1_ref, w3_ref, w2_ref)
    o_ref[...] = x
    h = _rms(x, g_ref[2:3, :]).astype(BF16)
    offs = [sum(_PROJ_WIDTHS[:i]) for i in range(len(_PROJ_WIDTHS) + 1)]
    proj = lambda i: jnp.dot(h, wp_ref[:, offs[i]:offs[i + 1]], preferred_element_type=F32)

    qkv = proj(0)
    ext = jnp.concatenate([cq_ref[...], qkv], axis=0)
    cq_ref[...] = qkv[tm - 8:tm, :]
    conv = qkv * pv_ref[GDN_CONV - 1:GDN_CONV, :]
    for j in range(1, GDN_CONV):
        conv = conv + ext[8 - j:8 - j + tm, :] * pv_ref[GDN_CONV - 1 - j:GDN_CONV - j, :]
    act = conv * _sigmoid(conv)
    qkv_ref[:, 0:GDN_WIDTH] = _l2_heads(act[:, 0:GDN_WIDTH], GDN_HEAD) * (GDN_HEAD ** -0.5)
    qkv_ref[:, GDN_WIDTH:2 * GDN_WIDTH] = _l2_heads(act[:, GDN_WIDTH:2 * GDN_WIDTH], GDN_HEAD)
    qkv_ref[:, 2 * GDN_WIDTH:3 * GDN_WIDTH] = act[:, 2 * GDN_WIDTH:3 * GDN_WIDTH]

    rw = proj(1)
    prev = jnp.concatenate([cr_ref[...], rw], axis=0)[7:7 + tm, :]
    cr_ref[...] = rw[tm - 8:tm, :]
    rw_ref[...] = rw + pv_ref[GDN_CONV:GDN_CONV + 1, 0:RW_COLS] * (prev - rw)

    gate = proj(2)
    gate_ref[...] = gate * _sigmoid(gate)
    s5_ref[...] = proj(3)
    small_ref[...] = proj(4)


def _mixout_ffn_body(x_ref, ya_ref, yb_ref, yc_ref, g_ref, wo_ref, w1_ref, w3_ref, w2_ref, o_ref):
    mixed = (jnp.dot(ya_ref[...].astype(BF16), wo_ref[0:256, :], preferred_element_type=F32)
             + jnp.dot(yb_ref[...].astype(BF16), wo_ref[256:512, :], preferred_element_type=F32)
             + jnp.dot(yc_ref[...].astype(BF16), wo_ref[512:1024, :], preferred_element_type=F32))
    x = x_ref[...] + _rms(mixed, g_ref[0:1, :])
    o_ref[...] = _ffn_core(x, g_ref[1:2, :], g_ref[2:3, :], w1_ref, w3_ref, w2_ref)


def _row_spec(width):
    return pl.BlockSpec((None, FFN_TM, width), lambda b, t: (b, t, 0))


def _ffn_inproj(x, gains, w1, w3, w2, wp, pvec):
    bsz, t_len, _ = x.shape
    sds = lambda wd: jax.ShapeDtypeStruct((bsz, t_len, wd), F32)
    return pl.pallas_call(
        _ffn_inproj_body,
        grid=(bsz, t_len // FFN_TM),
        in_specs=[_row_spec(D_MODEL)] + [_const_spec(a.shape) for a in (gains, w1, w3, w2, wp, pvec)],
        out_specs=[_row_spec(D_MODEL)] + [_row_spec(wd) for wd in _PROJ_WIDTHS],
        out_shape=[sds(D_MODEL)] + [sds(wd) for wd in _PROJ_WIDTHS],
        scratch_shapes=[pltpu.VMEM((8, _QKV_W), F32), pltpu.VMEM((8, RW_COLS), F32)],
        compiler_params=pltpu.CompilerParams(dimension_semantics=("parallel", "arbitrary"),
                                             vmem_limit_bytes=VMEM_LIMIT),
        name="ffn_inproj",
    )(x, gains, w1, w3, w2, wp, pvec)


def _mixout_ffn(x, ya, yb, yc, gains, wo, w1, w3, w2):
    bsz, t_len, _ = x.shape
    return pl.pallas_call(
        _mixout_ffn_body,
        grid=(bsz, t_len // FFN_TM),
        in_specs=[_row_spec(D_MODEL), _row_spec(RW_WIDTH), _row_spec(S5_WIDTH), _row_spec(GDN_WIDTH)]
        + [_const_spec(a.shape) for a in (gains, wo, w1, w3, w2)],
        out_specs=_row_spec(D_MODEL),
        out_shape=jax.ShapeDtypeStruct(x.shape, F32),
        compiler_params=pltpu.CompilerParams(dimension_semantics=("parallel", "parallel"),
                                             vmem_limit_bytes=VMEM_LIMIT),
        name="mixout_ffn",
    )(x, ya, yb, yc, gains, wo, w1, w3, w2)


def _mix_spec(width):
    return pl.BlockSpec((MIX_NB, MIX_TB, width), lambda b, t: (b, t, 0))


def _rwkv_body(has_vres, *refs):
    if has_vres:
        (p_ref, small_ref, vfirst_ref, vec_ref, wlr_ref, wg2_ref, wv2_ref, y_ref, h_ref) = refs
    else:
        (p_ref, vec_ref, wlr_ref, wg2_ref, y_ref, vout_ref, h_ref) = refs
    tb = MIX_TB
    width = RW_WIDTH
    n_chunks = tb // CHUNK

    @pl.when(pl.program_id(1) == 0)
    def _():
        h_ref[...] = jnp.zeros(h_ref.shape, F32)

    vec = lambda i: vec_ref[i:i + 1, :]
    head_ones = jnp.where(_block_ones(width, RW_HEAD), 1.0, 0.0).astype(BF16)
    lower = _chunk_lower(tb)
    ri = _iota((CHUNK, width), 0)
    ci = _iota((CHUNK, width), 1) % CHUNK
    strict = ci < ri
    incl = ci <= ri
    bdm = _block_ones(width, RW_HEAD)
    bd = lambda x: _head_block_diag(x, bdm)

    seqs = []
    for s in range(MIX_NB):
        z = p_ref[s]
        r = z[:, 0:256]
        k = z[:, 256:512]
        v = z[:, 512:768]
        lr_in = z[:, 768:896]
        lr_act = jnp.where(_iota((1, 128), 1) < 64, jnp.tanh(lr_in), lr_in)
        lr = _bdot(lr_act, wlr_ref[...])
        lw = -math.exp(-0.5) * _sigmoid(vec(0) + lr[:, 0:256])
        a = _sigmoid(vec(1) + lr[:, 256:512])
        g = _bdot(_sigmoid(z[:, 896:1024]), wg2_ref[...])
        if has_vres:
            mix = _sigmoid(vec(8) + _bdot(small_ref[s], wv2_ref[...]))
            v = v + (vfirst_ref[s] - v) * mix
        else:
            vout_ref[s] = v
        kkr = k * vec(2)
        kk = kkr * lax.rsqrt(jnp.maximum(_dot_exact_lhs(kkr * kkr, head_ones), L2_EPS * L2_EPS))
        k2 = k * (1.0 + (a - 1.0) * vec(3))
        bv = kk * a
        cl = _dot_exact_rhs(lower, lw)
        cl_end = _chunk_last_rows(cl)
        e_inv = jnp.exp(-cl)
        e_end = jnp.exp(cl_end - cl)
        seqs.append(dict(
            r=r, v=v, g=g, k2=k2,
            rt=r * jnp.exp(cl), at=-kk * jnp.exp(cl - lw), bt=bv * e_inv, kt=k2 * e_inv,
            bh=bv * e_end, kh=k2 * e_end, g_end=jnp.exp(cl_end)))

    items = [(s, c) for s in range(MIX_NB) for c in range(n_chunks)]
    pb, pk = {}, {}
    for s, c in items:
        q = seqs[s]
        sl = slice(c * CHUNK, (c + 1) * CHUNK)
        lhs = jnp.concatenate([q["at"][sl], q["rt"][sl]], axis=0).astype(BF16)
        pb[s, c] = lax.dot_general(lhs, bd(q["bt"][sl]), (((1,), (1,)), ((), ())), preferred_element_type=F32)
        pk[s, c] = lax.dot_general(lhs, bd(q["kt"][sl]), (((1,), (1,)), ((), ())), preferred_element_type=F32)
    tinv = dict(zip(items, _unit_lower_inverse_cat(
        [jnp.where(strict, pb[key][0:CHUNK], 0.0) for key in items], bdm)))
    rows = {(s, c): slice(c * CHUNK, (c + 1) * CHUNK) for s, c in items}
    sub = lambda key, name: seqs[key[0]][name][rows[key]]
    akv = {}
    for key in items:
        a_k = jnp.concatenate([jnp.where(strict, pk[key][0:CHUNK], 0.0),
                               jnp.where(incl, pk[key][CHUNK:2 * CHUNK], 0.0)], axis=0).astype(BF16)
        akv[key] = jnp.dot(a_k, bd(sub(key, "v")), preferred_element_type=F32)
    wu = {key: jnp.dot(tinv[key].astype(BF16),
                       jnp.concatenate([bd(sub(key, "at")), bd(akv[key][0:CHUNK])], axis=1),
                       preferred_element_type=F32) for key in items}
    rwu = {key: jnp.dot(jnp.where(incl, pb[key][CHUNK:2 * CHUNK], 0.0).astype(BF16),
                        jnp.concatenate([bd(wu[key][:, 0:width]), bd(wu[key][:, width:2 * width])], axis=1),
                        preferred_element_type=F32) for key in items}
    chunk = {}
    for key in items:
        wt, ut = wu[key][:, 0:width], wu[key][:, width:2 * width]
        m = jnp.where(bdm, _bdot_tn(sub(key, "bh"), wt), 0.0)
        chunk[key] = dict(
            m_ry=jnp.concatenate([m, sub(key, "rt") + rwu[key][:, 0:width]], axis=0).astype(BF16),
            y0=rwu[key][:, width:2 * width] + akv[key][CHUNK:2 * CHUNK],
            n0=jnp.where(bdm, _bdot_tn(jnp.concatenate([sub(key, "bh"), sub(key, "kh")], axis=0),
                                       jnp.concatenate([ut, sub(key, "v")], axis=0)), 0.0),
            gcol=jnp.transpose(jnp.concatenate([sub(key, "g_end")] * (width // CHUNK), axis=0)))

    states = [h_ref[s] for s in range(MIX_NB)]
    ys = [[] for _ in range(MIX_NB)]
    for c in range(n_chunks):
        for s in range(MIX_NB):
            d = chunk[s, c]
            prod = jnp.dot(d["m_ry"], states[s].astype(BF16), preferred_element_type=F32)
            ys[s].append(prod[width:width + CHUNK] + d["y0"])
            states[s] = d["gcol"] * states[s] + prod[0:width] + d["n0"]
    for s in range(MIX_NB):
        h_ref[s] = states[s]

    inv_n = 1.0 / RW_HEAD
    for s in range(MIX_NB):
        q = seqs[s]
        y = jnp.concatenate(ys[s], axis=0)
        yc = y - _dot_exact_lhs(y, head_ones) * inv_n
        var = _dot_exact_lhs(yc * yc, head_ones) * inv_n
        yn = yc * lax.rsqrt(var + RW_GN_EPS) * vec(5) + vec(6)
        yn = yn + _dot_exact_lhs(q["r"] * q["k2"] * vec(4), head_ones) * q["v"]
        y_ref[s] = yn * q["g"]


def _rwkv(p_rw, small, v_first, vecs, wlr, wg2, wv2):
    bsz, t_len, _ = p_rw.shape
    has_vres = v_first is not None
    y_sds = jax.ShapeDtypeStruct((bsz, t_len, RW_WIDTH), F32)
    if has_vres:
        args = (p_rw, small, v_first, vecs, wlr, wg2, wv2)
        in_specs = [_mix_spec(RW_COLS), _mix_spec(P_SMALL), _mix_spec(RW_WIDTH)] + [
            _const_spec(a.shape) for a in (vecs, wlr, wg2, wv2)]
        out_specs, out_shape = _mix_spec(RW_WIDTH), y_sds
    else:
        args = (p_rw, vecs, wlr, wg2)
        in_specs = [_mix_spec(RW_COLS)] + [_const_spec(a.shape) for a in (vecs, wlr, wg2)]
        out_specs, out_shape = [_mix_spec(RW_WIDTH), _mix_spec(RW_WIDTH)], [y_sds, y_sds]
    return pl.pallas_call(
        functools.partial(_rwkv_body, has_vres),
        grid=(bsz // MIX_NB, t_len // MIX_TB),
        in_specs=in_specs,
        out_specs=out_specs,
        out_shape=out_shape,
        scratch_shapes=[pltpu.VMEM((MIX_NB, RW_WIDTH, RW_WIDTH), F32)],
        compiler_params=pltpu.CompilerParams(dimension_semantics=("parallel", "arbitrary"),
                                             vmem_limit_bytes=VMEM_LIMIT),
        name="rwkv_vres" if has_vres else "rwkv",
    )(*args)


def _s5_disc_body(are_ref, aim_ref, ldt_ref, bre_ref, bim_ref, abar_re_ref, abar_im_ref, bbre_ref, bbim_ref):
    a_re = are_ref[...]
    a_im = aim_ref[...]
    dt = jnp.exp(ldt_ref[...])
    mag = jnp.exp(dt * a_re)
    ang = dt * a_im
    abar_re = mag * jnp.cos(ang)
    abar_im = mag * jnp.sin(ang)
    den = a_re * a_re + a_im * a_im
    num_re = abar_re - 1.0
    coef_re = (num_re * a_re + abar_im * a_im) / den
    coef_im = (abar_im * a_re - num_re * a_im) / den
    abar_re_ref[...] = abar_re
    abar_im_ref[...] = abar_im
    b_re = bre_ref[...]
    b_im = bim_ref[...]
    bbre_ref[...] = coef_re * b_re - coef_im * b_im
    bbim_ref[...] = coef_re * b_im + coef_im * b_re


def _s5_discretize(a_re, a_im, log_dt, b_re, b_im):
    n = S5_NSTATE
    col = lambda z: z.astype(F32).reshape(n, 1)
    ldt = jnp.broadcast_to(log_dt.astype(F32)[:, None], (S5_GROUPS, S5_STATE))
    mat = lambda z: z.reshape(n, S5_GROUP_DIM)
    return pl.pallas_call(
        _s5_disc_body,
        out_shape=[jax.ShapeDtypeStruct((n, 1), F32)] * 2 + [jax.ShapeDtypeStruct((n, S5_GROUP_DIM), F32)] * 2,
        name="s5_discretize",
    )(col(a_re), col(a_im), col(ldt), mat(b_re), mat(b_im))


def _gelu_tanh(x):
    return 0.5 * x * (1.0 + jnp.tanh(math.sqrt(2.0 / math.pi) * (x + 0.044715 * (x * x * x))))


def _s5_body(u_ref, abar_ref, bmat_ref, cmat_ref, vec_ref, wglu_ref, y_ref, x_ref, tm_ref, st_ref):
    n = S5_NSTATE
    nb = u_ref.shape[0]
    halves = S5_WIDTH // 128

    @pl.when(pl.program_id(0) == 0)
    def _():
        st_ref[...] = jnp.zeros(st_ref.shape, F32)

    for b in range(nb):
        ub = u_ref[b]
        for j in range(halves):
            tm_ref[j, pl.ds(b, S5_TB, stride=nb), :] = ub[:, j * 128:(j + 1) * 128]
    u = jnp.concatenate([tm_ref[j] for j in range(halves)], axis=1)
    x_ref[...] = _bdot(u, bmat_ref[...])
    a_re = jnp.broadcast_to(abar_ref[0:1, :], (nb, n))
    a_im = jnp.broadcast_to(abar_ref[1:2, :], (nb, n))

    def step(t, carry):
        s_re, s_im = carry
        rows = pl.ds(pl.multiple_of(t * nb, nb), nb)
        n_re = a_re * s_re - a_im * s_im + x_ref[rows, 0:n]
        n_im = a_re * s_im + a_im * s_re + x_ref[rows, n:2 * n]
        x_ref[rows, 0:n] = n_re
        x_ref[rows, n:2 * n] = n_im
        return n_re, n_im

    s_re, s_im = lax.fori_loop(0, S5_TB, step, (st_ref[:, 0:n], st_ref[:, n:2 * n]))
    st_ref[:, 0:n] = s_re
    st_ref[:, n:2 * n] = s_im

    y = _bdot(x_ref[...], cmat_ref[...]) + vec_ref[0:1, :] * u
    zz = _gelu_tanh(y)
    res = zz * _sigmoid(_bdot(zz, wglu_ref[...]) + vec_ref[1:2, :])
    for j in range(halves):
        tm_ref[j] = res[:, j * 128:(j + 1) * 128]
    for b in range(nb):
        y_ref[b] = jnp.concatenate([tm_ref[j, pl.ds(b, S5_TB, stride=nb), :] for j in range(halves)], axis=1)


def _s5(u, abar, bmat, cmat, vecs, wglu):
    bsz, t_len, _ = u.shape
    blk = pl.BlockSpec((bsz, S5_TB, S5_WIDTH), lambda t: (0, t, 0))
    return pl.pallas_call(
        _s5_body,
        grid=(t_len // S5_TB,),
        in_specs=[blk] + [_const_spec(a.shape) for a in (abar, bmat, cmat, vecs, wglu)],
        out_specs=blk,
        out_shape=jax.ShapeDtypeStruct(u.shape, F32),
        scratch_shapes=[pltpu.VMEM((S5_TB * bsz, 2 * S5_NSTATE), F32),
                        pltpu.VMEM((S5_WIDTH // 128, S5_TB * bsz, 128), F32),
                        pltpu.VMEM((bsz, 2 * S5_NSTATE), F32)],
        compiler_params=pltpu.CompilerParams(dimension_semantics=("arbitrary",),
                                             vmem_limit_bytes=VMEM_LIMIT),
        name="s5",
    )(u, abar, bmat, cmat, vecs, wglu)


def _gdn_body(qkv_ref, gate_ref, small_ref, vec_ref, o_ref, s_ref):
    tb = MIX_TB
    width = GDN_WIDTH
    n_chunks = tb // CHUNK
    cat = GDN_HEADS * CHUNK

    @pl.when(pl.program_id(1) == 0)
    def _():
        s_ref[...] = jnp.zeros(s_ref.shape, F32)

    lane_c = _iota((1, P_SMALL), 1)
    bc = lambda z, j: jnp.broadcast_to(z[:, j:j + 1], (tb, GDN_HEAD))
    lower = _chunk_lower(tb)
    upper = (_iota((tb, cat), 0) % CHUNK) > (_iota((tb, cat), 1) % CHUNK)
    ri = _iota((CHUNK, cat), 0)
    ci = _iota((CHUNK, cat), 1) % CHUNK
    strict = ci < ri
    incl = ci <= ri
    bdm = _block_ones(cat, CHUNK)
    bd_k = (_iota((cat, width), 0) // CHUNK) == (_iota((cat, width), 1) // GDN_HEAD)
    bd_sol = (_iota((cat, 2 * width), 0) // CHUNK) == (_iota((cat, 2 * width), 1) // (2 * GDN_HEAD))
    tile = lambda x: jnp.concatenate([x] * GDN_HEADS, axis=0)

    seqs = []
    for s in range(MIX_NB):
        x = qkv_ref[s]
        q = x[:, 0:width]
        k = x[:, width:2 * width]
        v = x[:, 2 * width:3 * width]
        cs = small_ref[s]
        gate_c = jnp.where(lane_c < GDN_HEADS, _sigmoid(cs),
                           -jnp.exp(vec_ref[0:1, 0:P_SMALL]) * _softplus(cs + vec_ref[1:2, 0:P_SMALL]))
        gam_c = _dot_exact_rhs(lower, gate_c)
        beta = jnp.concatenate([bc(gate_c, h) for h in range(GDN_HEADS)], axis=1)
        gam = jnp.concatenate([bc(gam_c, GDN_HEADS + h) for h in range(GDN_HEADS)], axis=1)
        g_cat = jnp.concatenate([jnp.where(lane_c < CHUNK, bc(gate_c, GDN_HEADS + h), bc(gate_c, GDN_HEADS + h + 1))
                                 for h in range(0, GDN_HEADS, 2)], axis=1)
        gam_end = _chunk_last_rows(gam)
        e_gam = jnp.exp(gam)
        k_beta = k * beta
        seqs.append(dict(
            q=q.astype(BF16), k=k.astype(BF16), k_beta=k_beta.astype(BF16),
            dec=jnp.exp(_dot_exact_rhs(lower, jnp.where(upper, g_cat, 0.0))),
            v_beta=(v * beta).astype(BF16), w_rhs=(k_beta * e_gam).astype(BF16),
            q_dec=q * e_gam, k_dec=(k * jnp.exp(gam_end - gam)).astype(BF16), c_dec=jnp.exp(gam_end)))

    items = [(s, c) for s in range(MIX_NB) for c in range(n_chunks)]
    ns, attn = {}, {}
    for s, c in items:
        d = seqs[s]
        sl = slice(c * CHUNK, (c + 1) * CHUNK)
        k_bd = jnp.where(bd_k, tile(d["k"][sl]), 0.0).astype(BF16)
        p1 = lax.dot_general(jnp.concatenate([d["k_beta"][sl], d["q"][sl]], axis=0), k_bd,
                             (((1,), (1,)), ((), ())), preferred_element_type=F32)
        dec = d["dec"][sl]
        ns[s, c] = jnp.where(strict, -(p1[0:CHUNK] * dec), 0.0)
        attn[s, c] = jnp.where(incl, p1[CHUNK:2 * CHUNK] * dec, 0.0).astype(BF16)
    tinv = dict(zip(items, _unit_lower_inverse_cat([ns[key] for key in items], bdm)))
    rows = {(s, c): slice(c * CHUNK, (c + 1) * CHUNK) for s, c in items}
    sol_all, qo_all = {}, {}
    for key in items:
        d = seqs[key[0]]
        rhs = jnp.concatenate([z[rows[key], h * GDN_HEAD:(h + 1) * GDN_HEAD] for h in range(GDN_HEADS)
                               for z in (d["v_beta"], d["w_rhs"])], axis=1)
        sol_all[key] = jnp.dot(tinv[key].astype(BF16), jnp.where(bd_sol, tile(rhs), 0.0).astype(BF16),
                               preferred_element_type=F32).astype(BF16)
    for key in items:
        qo_all[key] = jnp.dot(attn[key], jnp.where(bd_sol, tile(sol_all[key]), 0.0).astype(BF16),
                              preferred_element_type=F32)
    chunk = {}
    for s, c in items:
        d = seqs[s]
        sl = rows[s, c]
        for h in range(GDN_HEADS):
            hl = slice(h * GDN_HEAD, (h + 1) * GDN_HEAD)
            uw = slice(2 * h * GDN_HEAD, 2 * (h + 1) * GDN_HEAD)
            qo = qo_all[s, c][:, uw]
            mn = lax.dot_general(d["k_dec"][sl, hl], sol_all[s, c][:, uw], (((0,), (0,)), ((), ())),
                                 preferred_element_type=F32)
            chunk[s, c, h] = dict(
                m_qd=jnp.concatenate([mn[:, GDN_HEAD:2 * GDN_HEAD],
                                      d["q_dec"][sl, hl] - qo[:, GDN_HEAD:2 * GDN_HEAD]], axis=0).astype(BF16),
                o0=qo[:, 0:GDN_HEAD], n0=mn[:, 0:GDN_HEAD], c_dec=d["c_dec"][c * CHUNK:c * CHUNK + 1, hl])

    states = {(s, h): s_ref[s, h] for s in range(MIX_NB) for h in range(GDN_HEADS)}
    outs = {}
    for c in range(n_chunks):
        for s in range(MIX_NB):
            for h in range(GDN_HEADS):
                d = chunk[s, c, h]
                st = states[s, h]
                prod = jnp.dot(d["m_qd"], st.astype(BF16), preferred_element_type=F32)
                outs[s, c, h] = prod[GDN_HEAD:GDN_HEAD + CHUNK] + d["o0"]
                states[s, h] = st * d["c_dec"] - prod[0:GDN_HEAD] + d["n0"]
    for s in range(MIX_NB):
        for h in range(GDN_HEADS):
            s_ref[s, h] = states[s, h]

    for s in range(MIX_NB):
        o = jnp.concatenate([jnp.concatenate([outs[s, c, h] for h in range(GDN_HEADS)], axis=1)
                             for c in range(n_chunks)], axis=0)
        o_ref[s] = (o * lax.rsqrt(_head_sumsq(o, GDN_HEAD) * (1.0 / GDN_HEAD) + RMS_EPS) * vec_ref[2:3, :]
                    * gate_ref[s])


def _gdn(qkv, gate, small, vecs):
    bsz, t_len, _ = qkv.shape
    return pl.pallas_call(
        _gdn_body,
        grid=(bsz // MIX_NB, t_len // MIX_TB),
        in_specs=[_mix_spec(_QKV_W), _mix_spec(GDN_WIDTH), _mix_spec(P_SMALL), _const_spec(vecs.shape)],
        out_specs=_mix_spec(GDN_WIDTH),
        out_shape=jax.ShapeDtypeStruct((bsz, t_len, GDN_WIDTH), F32),
        scratch_shapes=[pltpu.VMEM((MIX_NB, GDN_HEADS, GDN_HEAD, GDN_HEAD), F32)],
        compiler_params=pltpu.CompilerParams(dimension_semantics=("parallel", "arbitrary"),
                                             vmem_limit_bytes=VMEM_LIMIT),
        name="gdn",
    )(qkv, gate, small, vecs)


def _pad_rows(w, rows, at):
    out = jnp.zeros((rows, w.shape[1]), w.dtype)
    return lax.dynamic_update_slice(out, w, (at, 0))


def _proj_weight(w_in_l, w_vres_l):
    o_s5 = RW_COLS
    o_qkv = o_s5 + S5_WIDTH
    o_gate = o_qkv + _QKV_W
    o_ba = o_gate + GDN_WIDTH
    small = [w_in_l[:, o_ba:o_ba + 2 * GDN_HEADS]]
    used = 2 * GDN_HEADS
    if w_vres_l is not None:
        small.append(w_vres_l)
        used += RW_V_RANK
    small.append(jnp.zeros((D_MODEL, P_SMALL - used), w_in_l.dtype))
    return jnp.concatenate([w_in_l[:, o_qkv:o_gate], w_in_l[:, 0:RW_COLS], w_in_l[:, o_gate:o_ba],
                            w_in_l[:, o_s5:o_qkv]] + small, axis=1).astype(BF16)


def _s5_matrices(abar_re, abar_im, bb_re, bb_im, c_re, c_im):
    g, n, hc = S5_GROUPS, S5_STATE, S5_GROUP_DIM
    eye = jnp.eye(g, dtype=F32)
    place_b = lambda bb: jnp.einsum("gnh,gk->ghkn", bb.reshape(g, n, hc), eye).reshape(g * hc, g * n)
    bmat = jnp.concatenate([place_b(bb_re), place_b(bb_im)], axis=1)
    place_c = lambda cc: jnp.einsum("ghn,gk->gnkh", cc, eye).reshape(g * n, g * hc)
    cmat = jnp.concatenate([place_c(c_re), -place_c(c_im)], axis=0)
    abar = jnp.concatenate([abar_re.reshape(1, g * n), abar_im.reshape(1, g * n)], axis=0)
    return abar, bmat.astype(BF16), cmat.astype(BF16)


def kernel(x, norm_gain, ffn_w1, ffn_w3, ffn_w2, w_in, w_in_vres, w_out, rwkv_mu, rwkv_w0, rwkv_w_w2, rwkv_a0, rwkv_w_a2, rwkv_w_g2, rwkv_k_k, rwkv_k_a, rwkv_r_k, rwkv_ln_w, rwkv_ln_b, rwkv_v0, rwkv_w_v2, s5_a_re, s5_a_im, s5_log_dt, s5_b_re, s5_b_im, s5_c_re, s5_c_im, s5_d, s5_w_glu, s5_b_glu, gdn_conv_w, gdn_a_log, gdn_dt_bias, gdn_norm_w):
    depth = norm_gain.shape[0]
    w1 = ffn_w1.astype(BF16)
    w3 = ffn_w3.astype(BF16)
    w2 = ffn_w2.astype(BF16)
    x = x.astype(F32)
    v_first = None
    for l in range(depth):
        w_proj = _proj_weight(w_in[l], None if l == 0 else w_in_vres[l - 1])
        pvec = jnp.zeros((8, _QKV_W), F32).at[0:GDN_CONV].set(gdn_conv_w[l].astype(F32))
        pvec = pvec.at[GDN_CONV, 0:RW_COLS].set(rwkv_mu[l].astype(F32))
        x, p_qkv, p_rw, p_gate, p_s5, p_small = _ffn_inproj(
            x, norm_gain[l, 0:3], w1[l, 0], w3[l, 0], w2[l, 0], w_proj, pvec)

        zero = jnp.zeros((RW_WIDTH,), F32)
        vecs = jnp.stack([
            rwkv_w0[l], rwkv_a0[l], rwkv_k_k[l], rwkv_k_a[l], rwkv_r_k[l].reshape(RW_WIDTH),
            rwkv_ln_w[l], rwkv_ln_b[l], zero, rwkv_v0[l - 1] if l > 0 else zero, zero,
            zero, zero, zero, zero, zero, zero]).astype(F32)
        wlr = jnp.zeros((128, 512), F32)
        wlr = wlr.at[0:64, 0:256].set(rwkv_w_w2[l]).at[64:128, 256:512].set(rwkv_w_a2[l]).astype(BF16)
        wg2 = rwkv_w_g2[l].astype(BF16)
        if l == 0:
            y_a, v_first = _rwkv(p_rw, None, None, vecs, wlr, wg2, None)
        else:
            wv2 = _pad_rows(rwkv_w_v2[l - 1], P_SMALL, 2 * GDN_HEADS).astype(BF16)
            y_a = _rwkv(p_rw, p_small, v_first, vecs, wlr, wg2, wv2)

        abar_re, abar_im, bb_re, bb_im = _s5_discretize(s5_a_re[l], s5_a_im[l], s5_log_dt[l], s5_b_re[l], s5_b_im[l])
        abar, bmat, cmat = _s5_matrices(abar_re, abar_im, bb_re, bb_im, s5_c_re[l], s5_c_im[l])
        s5_vecs = jnp.zeros((8, S5_WIDTH), F32).at[0].set(s5_d[l]).at[1].set(s5_b_glu[l])
        y_b = _s5(p_s5, abar, bmat, cmat, s5_vecs, s5_w_glu[l].astype(BF16))

        compact = lambda z: jnp.zeros((GDN_WIDTH,), F32).at[GDN_HEADS:2 * GDN_HEADS].set(z.astype(F32))
        gdn_vecs = jnp.zeros((8, GDN_WIDTH), F32)
        gdn_vecs = gdn_vecs.at[0].set(compact(gdn_a_log[l])).at[1].set(compact(gdn_dt_bias[l]))
        gdn_vecs = gdn_vecs.at[2].set(jnp.tile(gdn_norm_w[l], GDN_HEADS))
        y_c = _gdn(p_qkv, p_gate, p_small, gdn_vecs)

        x = _mixout_ffn(x, y_a, y_b, y_c, norm_gain[l, 3:6], w_out[l].astype(BF16), w1[l, 1], w3[l, 1], w2[l, 1])
    return x
```

```python
import functools
import math

import jax
import jax.numpy as jnp
from jax import lax
from jax.experimental import pallas as pl
from jax.experimental.pallas import tpu as pltpu

F32 = jnp.float32
BF16 = jnp.bfloat16

D_MODEL = 1024
D_FF = 2816
RMS_EPS = 1e-6
L2_EPS = 1e-12

RW_WIDTH = 256
RW_HEAD = 64
RW_HEADS = 4
RW_GN_EPS = 64e-5
RW_COLS = 1024
RW_V_RANK = 32

S5_WIDTH = 256
S5_GROUPS = 16
S5_GROUP_DIM = 16
S5_STATE = 64
S5_NSTATE = S5_GROUPS * S5_STATE

GDN_WIDTH = 512
GDN_HEAD = 128
GDN_HEADS = 4
GDN_CONV = 4

P_SMALL = 128

CHUNK = 64
MIX_TB = 256
MIX_NB = 2
S5_TB = 128
S5_ROWS = 256
FFN_TM = 512
FFN_TF = 256
VMEM_LIMIT = 56 * 1024 * 1024


def _bdot(a, b):
    return jnp.dot(a.astype(BF16), b.astype(BF16), preferred_element_type=F32)


def _bdot_tn(a, b):
    return lax.dot_general(a.astype(BF16), b.astype(BF16), (((0,), (0,)), ((), ())),
                           preferred_element_type=F32)


def _split(x, n):
    pieces = []
    r = x
    for i in range(n):
        p = r.astype(BF16)
        pieces.append(p)
        if i + 1 < n:
            r = r - p.astype(F32)
    return pieces


def _dot_exact_rhs(w01, x, n=2):
    out = None
    for p in _split(x, n):
        t = jnp.dot(w01, p, preferred_element_type=F32)
        out = t if out is None else out + t
    return out


def _head_sums(x, head):
    assert 2 * head == 128
    low = _iota((1, 128), 1) < head
    parts = []
    for c in range(x.shape[1] // 128):
        xc = x[:, c * 128:(c + 1) * 128]
        s_lo = jnp.sum(jnp.where(low, xc, 0.0), axis=-1, keepdims=True)
        s_hi = jnp.sum(jnp.where(low, 0.0, xc), axis=-1, keepdims=True)
        parts.append(jnp.where(low, s_lo, s_hi))
    return jnp.concatenate(parts, axis=1)


def _rms(z, gain):
    ms = jnp.mean(z * z, axis=-1, keepdims=True)
    return z * lax.rsqrt(ms + RMS_EPS) * gain


def _sigmoid(x):
    return 1.0 / (1.0 + jnp.exp2(x * (-1.0 / math.log(2.0))))


def _softplus(x):
    return jnp.maximum(x, 0.0) + jnp.log1p(jnp.exp(-jnp.abs(x)))


def _iota(shape, dim):
    return lax.broadcasted_iota(jnp.int32, shape, dim)


def _chunk_lower(tb):
    r = _iota((tb, tb), 0)
    c = _iota((tb, tb), 1)
    return jnp.where(((r // CHUNK) == (c // CHUNK)) & (c <= r), 1.0, 0.0).astype(BF16)


def _chunk_last_rows(x):
    parts = []
    for c in range(x.shape[0] // CHUNK):
        last = x[(c + 1) * CHUNK - 1:(c + 1) * CHUNK, :]
        parts.append(jnp.broadcast_to(last, (CHUNK, x.shape[1])))
    return jnp.concatenate(parts, axis=0)


def _block_ones(n, blk):
    r = _iota((n, n), 0)
    c = _iota((n, n), 1)
    return (r // blk) == (c // blk)


def _head_block_diag(x, bdm):
    reps = bdm.shape[0] // x.shape[0]
    return jnp.where(bdm, jnp.concatenate([x] * reps, axis=0), 0.0).astype(BF16)


def _unit_lower_inverse_cat(ns, bdm):
    width = ns[0].shape[1]
    eye = jnp.where(_iota((CHUNK, width), 1) % CHUNK == _iota((CHUNK, width), 0), 1.0, 0.0).astype(F32)
    ts = [eye + n for n in ns]
    ps = [jnp.dot(n.astype(BF16), _head_block_diag(n, bdm), preferred_element_type=F32) for n in ns]
    levels = int(math.log2(CHUNK)) - 1
    for lvl in range(levels):
        new_ts, new_ps = [], []
        for t, p in zip(ts, ps):
            w = _head_block_diag(p, bdm)
            if lvl == levels - 1:
                new_ts.append(t + jnp.dot(t.astype(BF16), w, preferred_element_type=F32))
            else:
                both = jnp.dot(jnp.concatenate([t, p], axis=0).astype(BF16), w, preferred_element_type=F32)
                new_ts.append(t + both[0:CHUNK])
                new_ps.append(both[CHUNK:2 * CHUNK])
        ts, ps = new_ts, new_ps
    return ts


def _pick(arr, *lead):
    rest = arr.shape[len(lead):]
    return pl.BlockSpec((None,) * len(lead) + rest, lambda *_: tuple(lead) + (0,) * len(rest),
                        pipeline_mode=pl.Buffered(1))


_QKV_W = 3 * GDN_WIDTH
_PROJ_WIDTHS = (_QKV_W, RW_COLS, GDN_WIDTH, S5_WIDTH, P_SMALL)


def _head_sumsq(z, head):
    parts = []
    for h in range(z.shape[1] // head):
        zh = z[:, h * head:(h + 1) * head]
        parts.append(jnp.broadcast_to(jnp.sum(zh * zh, axis=-1, keepdims=True), zh.shape))
    return jnp.concatenate(parts, axis=1)


def _l2_heads(z, head):
    return z * lax.rsqrt(jnp.maximum(_head_sumsq(z, head), L2_EPS * L2_EPS))


def _ffn_core(x, gi, go, w1_ref, w3_ref, w2_ref):
    h = _rms(x, gi).astype(BF16)
    acc = jnp.zeros(x.shape, F32)
    for f in range(D_FF // FFN_TF):
        cs = slice(f * FFN_TF, (f + 1) * FFN_TF)
        a = jnp.dot(h, w1_ref[:, cs], preferred_element_type=F32)
        b = jnp.dot(h, w3_ref[:, cs], preferred_element_type=F32)
        act = (a * _sigmoid(a) * b).astype(BF16)
        acc = acc + jnp.dot(act, w2_ref[cs, :], preferred_element_type=F32)
    return x + 0.5 * _rms(acc, go)


def _ffn_inproj_body(x_ref, g_ref, w1_ref, w3_ref, w2_ref, wp_ref, pv_ref, o_ref,
                     qkv_ref, rw_ref, gate_ref, s5_ref, small_ref, cq_ref, cr_ref):
    tm = FFN_TM

    @pl.when(pl.program_id(1) == 0)
    def _():
        cq_ref[...] = jnp.zeros(cq_ref.shape, F32)
        cr_ref[...] = jnp.zeros(cr_ref.shape, F32)

    x = _ffn_core(x_ref[...], g_ref[0:1, :], g_ref[1:2, :], w1_ref, w3_ref, w2_ref)
    o_ref[...] = x
    h = _rms(x, g_ref[2:3, :]).astype(BF16)
    offs = [sum(_PROJ_WIDTHS[:i]) for i in range(len(_PROJ_WIDTHS) + 1)]
    proj = lambda i: jnp.dot(h, wp_ref[:, offs[i]:offs[i + 1]], preferred_element_type=F32)

    qkv = proj(0)
    ext = jnp.concatenate([cq_ref[...], qkv], axis=0)
    cq_ref[...] = qkv[tm - 8:tm, :]
    conv = qkv * pv_ref[GDN_CONV - 1:GDN_CONV, :]
    for j in range(1, GDN_CONV):
        conv = conv + ext[8 - j:8 - j + tm, :] * pv_ref[GDN_CONV - 1 - j:GDN_CONV - j, :]
    act = conv * _sigmoid(conv)
    qkv_ref[:, 0:GDN_WIDTH] = _l2_heads(act[:, 0:GDN_WIDTH], GDN_HEAD) * (GDN_HEAD ** -0.5)
    qkv_ref[:, GDN_WIDTH:2 * GDN_WIDTH] = _l2_heads(act[:, GDN_WIDTH:2 * GDN_WIDTH], GDN_HEAD)
    qkv_ref[:, 2 * GDN_WIDTH:3 * GDN_WIDTH] = act[:, 2 * GDN_WIDTH:3 * GDN_WIDTH]

    rw = proj(1)
    prev = jnp.concatenate([cr_ref[...], rw], axis=0)[7:7 + tm, :]
    cr_ref[...] = rw[tm - 8:tm, :]
    rw_ref[...] = rw + pv_ref[GDN_CONV:GDN_CONV + 1, 0:RW_COLS] * (prev - rw)

    gate = proj(2)
    gate_ref[...] = gate * _sigmoid(gate)
    s5_ref[...] = proj(3)
    small_ref[...] = proj(4)


def _mixout_ffn_body(x_ref, ya_ref, yb_ref, yc_ref, g_ref, wo_ref, w1_ref, w3_ref, w2_ref, o_ref):
    mixed = (jnp.dot(ya_ref[...].astype(BF16), wo_ref[0:256, :], preferred_element_type=F32)
             + jnp.dot(yb_ref[...].astype(BF16), wo_ref[256:512, :], preferred_element_type=F32)
             + jnp.dot(yc_ref[...].astype(BF16), wo_ref[512:1024, :], preferred_element_type=F32))
    x = x_ref[...] + _rms(mixed, g_ref[3:4, :])
    o_ref[...] = _ffn_core(x, g_ref[4:5, :], g_ref[5:6, :], w1_ref, w3_ref, w2_ref)


def _row_spec(width):
    return pl.BlockSpec((None, FFN_TM, width), lambda b, t: (b, t, 0))


def _ffn_inproj(x, l, gains, w1, w3, w2, wp, pvec):
    bsz, t_len, _ = x.shape
    sds = lambda wd: jax.ShapeDtypeStruct((bsz, t_len, wd), F32)
    return pl.pallas_call(
        _ffn_inproj_body,
        grid=(bsz, t_len // FFN_TM),
        in_specs=[_row_spec(D_MODEL), _pick(gains, l), _pick(w1, l, 0), _pick(w3, l, 0), _pick(w2, l, 0),
                  _pick(wp, l), _pick(pvec, l)],
        out_specs=[_row_spec(D_MODEL)] + [_row_spec(wd) for wd in _PROJ_WIDTHS],
        out_shape=[sds(D_MODEL)] + [sds(wd) for wd in _PROJ_WIDTHS],
        scratch_shapes=[pltpu.VMEM((8, _QKV_W), F32), pltpu.VMEM((8, RW_COLS), F32)],
        compiler_params=pltpu.CompilerParams(dimension_semantics=("parallel", "arbitrary"),
                                             vmem_limit_bytes=VMEM_LIMIT),
        name="ffn_inproj",
    )(x, gains, w1, w3, w2, wp, pvec)


def _mixout_ffn(x, ya, yb, yc, l, gains, wo, w1, w3, w2):
    bsz, t_len, _ = x.shape
    return pl.pallas_call(
        _mixout_ffn_body,
        grid=(bsz, t_len // FFN_TM),
        in_specs=[_row_spec(D_MODEL), _row_spec(RW_WIDTH), _row_spec(S5_WIDTH), _row_spec(GDN_WIDTH),
                  _pick(gains, l), _pick(wo, l), _pick(w1, l, 1), _pick(w3, l, 1), _pick(w2, l, 1)],
        out_specs=_row_spec(D_MODEL),
        out_shape=jax.ShapeDtypeStruct(x.shape, F32),
        compiler_params=pltpu.CompilerParams(dimension_semantics=("parallel", "parallel"),
                                             vmem_limit_bytes=VMEM_LIMIT),
        name="mixout_ffn",
    )(x, ya, yb, yc, gains, wo, w1, w3, w2)


def _mix_spec(width):
    return pl.BlockSpec((MIX_NB, MIX_TB, width), lambda b, t: (b, t, 0))


def _rwkv_body(has_vres, *refs):
    if has_vres:
        (p_ref, small_ref, vfirst_ref, vec_ref, wlr_ref, wg2_ref, wv2_ref, y_ref, h_ref) = refs
    else:
        (p_ref, vec_ref, wlr_ref, wg2_ref, y_ref, vout_ref, h_ref) = refs
    tb = MIX_TB
    width = RW_WIDTH
    n_chunks = tb // CHUNK

    @pl.when(pl.program_id(1) == 0)
    def _():
        h_ref[...] = jnp.zeros(h_ref.shape, F32)

    vec = lambda i: vec_ref[i:i + 1, :]
    lower = _chunk_lower(tb)
    ri = _iota((CHUNK, width), 0)
    ci = _iota((CHUNK, width), 1) % CHUNK
    strict = ci < ri
    incl = ci <= ri
    bdm = _block_ones(width, RW_HEAD)
    bd = lambda x: _head_block_diag(x, bdm)

    seqs = []
    for s in range(MIX_NB):
        z = p_ref[s]
        r = z[:, 0:256]
        k = z[:, 256:512]
        v = z[:, 512:768]
        lr_in = z[:, 768:896]
        lr_act = jnp.where(_iota((1, 128), 1) < 64, jnp.tanh(lr_in), lr_in)
        lr = _bdot(lr_act, wlr_ref[...])
        lw = -math.exp(-0.5) * _sigmoid(vec(0) + lr[:, 0:256])
        a = _sigmoid(vec(1) + lr[:, 256:512])
        g = _bdot(_sigmoid(z[:, 896:1024]), wg2_ref[...])
        if has_vres:
            mix = _sigmoid(vec(8) + _bdot(small_ref[s], wv2_ref[...]))
            v = v + (vfirst_ref[s] - v) * mix
        else:
            vout_ref[s] = v
        kkr = k * vec(2)
        kk = kkr * lax.rsqrt(jnp.maximum(_head_sums(kkr * kkr, RW_HEAD), L2_EPS * L2_EPS))
        k2 = k * (1.0 + (a - 1.0) * vec(3))
        bv = kk * a
        cl = _dot_exact_rhs(lower, lw)
        cl_end = _chunk_last_rows(cl)
        e_inv = jnp.exp(-cl)
        e_end = jnp.exp(cl_end - cl)
        seqs.append(dict(
            r=r, v=v, g=g, k2=k2,
            rt=r * jnp.exp(cl), at=-kk * jnp.exp(cl - lw), bt=bv * e_inv, kt=k2 * e_inv,
            bh=bv * e_end, kh=k2 * e_end, g_end=jnp.exp(cl_end)))

    items = [(s, c) for s in range(MIX_NB) for c in range(n_chunks)]
    pb, pk = {}, {}
    for s, c in items:
        q = seqs[s]
        sl = slice(c * CHUNK, (c + 1) * CHUNK)
        lhs = jnp.concatenate([q["at"][sl], q["rt"][sl]], axis=0).astype(BF16)
        pb[s, c] = lax.dot_general(lhs, bd(q["bt"][sl]), (((1,), (1,)), ((), ())), preferred_element_type=F32)
        pk[s, c] = lax.dot_general(lhs, bd(q["kt"][sl]), (((1,), (1,)), ((), ())), preferred_element_type=F32)
    tinv = dict(zip(items, _unit_lower_inverse_cat(
        [jnp.where(strict, pb[key][0:CHUNK], 0.0) for key in items], bdm)))
    rows = {(s, c): slice(c * CHUNK, (c + 1) * CHUNK) for s, c in items}
    sub = lambda key, name: seqs[key[0]][name][rows[key]]
    akv = {}
    for key in items:
        a_k = jnp.concatenate([jnp.where(strict, pk[key][0:CHUNK], 0.0),
                               jnp.where(incl, pk[key][CHUNK:2 * CHUNK], 0.0)], axis=0).astype(BF16)
        akv[key] = jnp.dot(a_k, bd(sub(key, "v")), preferred_element_type=F32)
    wu = {key: jnp.dot(tinv[key].astype(BF16),
                       jnp.concatenate([bd(sub(key, "at")), bd(akv[key][0:CHUNK])], axis=1),
                       preferred_element_type=F32) for key in items}
    rwu = {key: jnp.dot(jnp.where(incl, pb[key][CHUNK:2 * CHUNK], 0.0).astype(BF16),
                        jnp.concatenate([bd(wu[key][:, 0:width]), bd(wu[key][:, width:2 * width])], axis=1),
                        preferred_element_type=F32) for key in items}
    chunk = {}
    for key in items:
        wt, ut = wu[key][:, 0:width], wu[key][:, width:2 * width]
        m = jnp.where(bdm, _bdot_tn(sub(key, "bh"), wt), 0.0)
        chunk[key] = dict(
            m_ry=jnp.concatenate([m, sub(key, "rt") + rwu[key][:, 0:width]], axis=0).astype(BF16),
            y0=rwu[key][:, width:2 * width] + akv[key][CHUNK:2 * CHUNK],
            n0=jnp.where(bdm, _bdot_tn(jnp.concatenate([sub(key, "bh"), sub(key, "kh")], axis=0),
                                       jnp.concatenate([ut, sub(key, "v")], axis=0)), 0.0),
            gcol=jnp.transpose(jnp.concatenate([sub(key, "g_end")] * (width // CHUNK), axis=0)))

    states = [h_ref[s] for s in range(MIX_NB)]
    ys = [[] for _ in range(MIX_NB)]
    for c in range(n_chunks):
        for s in range(MIX_NB):
            d = chunk[s, c]
            prod = jnp.dot(d["m_ry"], states[s].astype(BF16), preferred_element_type=F32)
            ys[s].append(prod[width:width + CHUNK] + d["y0"])
            states[s] = d["gcol"] * states[s] + prod[0:width] + d["n0"]
    for s in range(MIX_NB):
        h_ref[s] = states[s]

    inv_n = 1.0 / RW_HEAD
    for s in range(MIX_NB):
        q = seqs[s]
        y = jnp.concatenate(ys[s], axis=0)
        yc = y - _head_sums(y, RW_HEAD) * inv_n
        var = _head_sums(yc * yc, RW_HEAD) * inv_n
        yn = yc * lax.rsqrt(var + RW_GN_EPS) * vec(5) + vec(6)
        yn = yn + _head_sums(q["r"] * q["k2"] * vec(4), RW_HEAD) * q["v"]
        y_ref[s] = yn * q["g"]


def _rwkv(p_rw, small, v_first, l, vecs, wlr, wg2, wv2):
    bsz, t_len, _ = p_rw.shape
    has_vres = v_first is not None
    y_sds = jax.ShapeDtypeStruct((bsz, t_len, RW_WIDTH), F32)
    if has_vres:
        args = (p_rw, small, v_first, vecs, wlr, wg2, wv2)
        in_specs = [_mix_spec(RW_COLS), _mix_spec(P_SMALL), _mix_spec(RW_WIDTH)] + [
            _pick(a, l) for a in (vecs, wlr, wg2, wv2)]
        out_specs, out_shape = _mix_spec(RW_WIDTH), y_sds
    else:
        args = (p_rw, vecs, wlr, wg2)
        in_specs = [_mix_spec(RW_COLS)] + [_pick(a, l) for a in (vecs, wlr, wg2)]
        out_specs, out_shape = [_mix_spec(RW_WIDTH), _mix_spec(RW_WIDTH)], [y_sds, y_sds]
    return pl.pallas_call(
        functools.partial(_rwkv_body, has_vres),
        grid=(bsz // MIX_NB, t_len // MIX_TB),
        in_specs=in_specs,
        out_specs=out_specs,
        out_shape=out_shape,
        scratch_shapes=[pltpu.VMEM((MIX_NB, RW_WIDTH, RW_WIDTH), F32)],
        compiler_params=pltpu.CompilerParams(dimension_semantics=("parallel", "arbitrary"),
                                             vmem_limit_bytes=VMEM_LIMIT),
        name="rwkv_vres" if has_vres else "rwkv",
    )(*args)


def _s5_disc_body(are_ref, aim_ref, ldt_ref, bre_ref, bim_ref, abar_re_ref, abar_im_ref, bbre_ref, bbim_ref):
    a_re = are_ref[...]
    a_im = aim_ref[...]
    dt = jnp.exp(ldt_ref[...])
    mag = jnp.exp(dt * a_re)
    ang = dt * a_im
    abar_re = mag * jnp.cos(ang)
    abar_im = mag * jnp.sin(ang)
    den = a_re * a_re + a_im * a_im
    num_re = abar_re - 1.0
    coef_re = (num_re * a_re + abar_im * a_im) / den
    coef_im = (abar_im * a_re - num_re * a_im) / den
    abar_re_ref[...] = abar_re
    abar_im_ref[...] = abar_im
    b_re = bre_ref[...]
    b_im = bim_ref[...]
    bbre_ref[...] = coef_re * b_re - coef_im * b_im
    bbim_ref[...] = coef_re * b_im + coef_im * b_re


def _s5_discretize(a_re, a_im, log_dt, b_re, b_im):
    n = a_re.shape[0] * S5_NSTATE
    col = lambda z: z.astype(F32).reshape(n, 1)
    ldt = jnp.broadcast_to(log_dt.astype(F32)[:, :, None], a_re.shape)
    mat = lambda z: z.astype(F32).reshape(n, S5_GROUP_DIM)
    return pl.pallas_call(
        _s5_disc_body,
        out_shape=[jax.ShapeDtypeStruct((n, 1), F32)] * 2 + [jax.ShapeDtypeStruct((n, S5_GROUP_DIM), F32)] * 2,
        name="s5_discretize",
    )(col(a_re), col(a_im), col(ldt), mat(b_re), mat(b_im))


def _gelu_tanh(x):
    return 0.5 * x * (1.0 + jnp.tanh(math.sqrt(2.0 / math.pi) * (x + 0.044715 * (x * x * x))))


def _s5_body(u_ref, abar_ref, bmat_ref, cmat_ref, vec_ref, wglu_ref, y_ref, x_ref, tm_ref, st_ref):
    n = S5_NSTATE
    nb = u_ref.shape[0]
    halves = S5_WIDTH // 128

    @pl.when(pl.program_id(0) == 0)
    def _():
        st_ref[...] = jnp.zeros(st_ref.shape, F32)

    for b in range(nb):
        ub = u_ref[b]
        for j in range(halves):
            tm_ref[j, pl.ds(b, S5_TB, stride=nb), :] = ub[:, j * 128:(j + 1) * 128]
    u = jnp.concatenate([tm_ref[j] for j in range(halves)], axis=1)
    n_rows = S5_TB * nb
    row_blocks = [slice(r, r + S5_ROWS) for r in range(0, n_rows, S5_ROWS)]
    for rs in row_blocks:
        x_ref[rs, :] = _bdot(u[rs], bmat_ref[...])
    a_re = jnp.broadcast_to(abar_ref[0:1, :], (nb, n))
    a_im = jnp.broadcast_to(abar_ref[1:2, :], (nb, n))

    def step(t, carry):
        s_re, s_im = carry
        rows = pl.ds(pl.multiple_of(t * nb, nb), nb)
        n_re = a_re * s_re - a_im * s_im + x_ref[rows, 0:n]
        n_im = a_re * s_im + a_im * s_re + x_ref[rows, n:2 * n]
        x_ref[rows, 0:n] = n_re
        x_ref[rows, n:2 * n] = n_im
        return n_re, n_im

    s_re, s_im = lax.fori_loop(0, S5_TB, step, (st_ref[:, 0:n], st_ref[:, n:2 * n]))
    st_ref[:, 0:n] = s_re
    st_ref[:, n:2 * n] = s_im

    for rs in row_blocks:
        y = _bdot(x_ref[rs, :], cmat_ref[...]) + vec_ref[0:1, :] * u[rs]
        zz = _gelu_tanh(y)
        res = zz * _sigmoid(_bdot(zz, wglu_ref[...]) + vec_ref[1:2, :])
        for j in range(halves):
            tm_ref[j, rs, :] = res[:, j * 128:(j + 1) * 128]
    for b in range(nb):
        y_ref[b] = jnp.concatenate([tm_ref[j, pl.ds(b, S5_TB, stride=nb), :] for j in range(halves)], axis=1)


def _s5(u, l, abar, bmat, cmat, vecs, wglu):
    bsz, t_len, _ = u.shape
    blk = pl.BlockSpec((bsz, S5_TB, S5_WIDTH), lambda t: (0, t, 0))
    return pl.pallas_call(
        _s5_body,
        grid=(t_len // S5_TB,),
        in_specs=[blk] + [_pick(a, l) for a in (abar, bmat, cmat, vecs, wglu)],
        out_specs=blk,
        out_shape=jax.ShapeDtypeStruct(u.shape, F32),
        scratch_shapes=[pltpu.VMEM((S5_TB * bsz, 2 * S5_NSTATE), F32),
                        pltpu.VMEM((S5_WIDTH // 128, S5_TB * bsz, 128), F32),
                        pltpu.VMEM((bsz, 2 * S5_NSTATE), F32)],
        compiler_params=pltpu.CompilerParams(dimension_semantics=("arbitrary",),
                                             vmem_limit_bytes=VMEM_LIMIT),
        name="s5",
    )(u, abar, bmat, cmat, vecs, wglu)


def _gdn_body(qkv_ref, gate_ref, small_ref, vec_ref, o_ref, s_ref):
    tb = MIX_TB
    width = GDN_WIDTH
    n_chunks = tb // CHUNK
    cat = GDN_HEADS * CHUNK

    @pl.when(pl.program_id(1) == 0)
    def _():
        s_ref[...] = jnp.zeros(s_ref.shape, F32)

    lane_c = _iota((1, P_SMALL), 1)
    bc = lambda z, j: jnp.broadcast_to(z[:, j:j + 1], (tb, GDN_HEAD))
    lower = _chunk_lower(tb)
    upper = (_iota((tb, cat), 0) % CHUNK) > (_iota((tb, cat), 1) % CHUNK)
    ri = _iota((CHUNK, cat), 0)
    ci = _iota((CHUNK, cat), 1) % CHUNK
    strict = ci < ri
    incl = ci <= ri
    bdm = _block_ones(cat, CHUNK)
    bd_k = (_iota((cat, width), 0) // CHUNK) == (_iota((cat, width), 1) // GDN_HEAD)
    bd_sol = (_iota((cat, 2 * width), 0) // CHUNK) == (_iota((cat, 2 * width), 1) // (2 * GDN_HEAD))
    tile = lambda x: jnp.concatenate([x] * GDN_HEADS, axis=0)

    seqs = []
    for s in range(MIX_NB):
        x = qkv_ref[s]
        q = x[:, 0:width]
        k = x[:, width:2 * width]
        v = x[:, 2 * width:3 * width]
        cs = small_ref[s]
        gate_c = jnp.where(lane_c < GDN_HEADS, _sigmoid(cs),
                           -jnp.exp(vec_ref[0:1, 0:P_SMALL]) * _softplus(cs + vec_ref[1:2, 0:P_SMALL]))
        gam_c = _dot_exact_rhs(lower, gate_c)
        beta = jnp.concatenate([bc(gate_c, h) for h in range(GDN_HEADS)], axis=1)
        gam = jnp.concatenate([bc(gam_c, GDN_HEADS + h) for h in range(GDN_HEADS)], axis=1)
        g_cat = jnp.concatenate([jnp.where(lane_c < CHUNK, bc(gate_c, GDN_HEADS + h), bc(gate_c, GDN_HEADS + h + 1))
                                 for h in range(0, GDN_HEADS, 2)], axis=1)
        gam_end = _chunk_last_rows(gam)
        e_gam = jnp.exp(gam)
        k_beta = k * beta
        seqs.append(dict(
            q=q.astype(BF16), k=k.astype(BF16), k_beta=k_beta.astype(BF16),
            dec=jnp.exp(_dot_exact_rhs(lower, jnp.where(upper, g_cat, 0.0))),
            v_beta=(v * beta).astype(BF16), w_rhs=(k_beta * e_gam).astype(BF16),
            q_dec=q * e_gam, k_dec=(k * jnp.exp(gam_end - gam)).astype(BF16), c_dec=jnp.exp(gam_end)))

    items = [(s, c) for s in range(MIX_NB) for c in range(n_chunks)]
    ns, attn = {}, {}
    for s, c in items:
        d = seqs[s]
        sl = slice(c * CHUNK, (c + 1) * CHUNK)
        k_bd = jnp.where(bd_k, tile(d["k"][sl]), 0.0).astype(BF16)
        p1 = lax.dot_general(jnp.concatenate([d["k_beta"][sl], d["q"][sl]], axis=0), k_bd,
                             (((1,), (1,)), ((), ())), preferred_element_type=F32)
        dec = d["dec"][sl]
        ns[s, c] = jnp.where(strict, -(p1[0:CHUNK] * dec), 0.0)
        attn[s, c] = jnp.where(incl, p1[CHUNK:2 * CHUNK] * dec, 0.0).astype(BF16)
    tinv = dict(zip(items, _unit_lower_inverse_cat([ns[key] for key in items], bdm)))
    rows = {(s, c): slice(c * CHUNK, (c + 1) * CHUNK) for s, c in items}
    sol_all, qo_all = {}, {}
    for key in items:
        d = seqs[key[0]]
        rhs = jnp.concatenate([z[rows[key], h * GDN_HEAD:(h + 1) * GDN_HEAD] for h in range(GDN_HEADS)
                               for z in (d["v_beta"], d["w_rhs"])], axis=1)
        sol_all[key] = jnp.dot(tinv[key].astype(BF16), jnp.where(bd_sol, tile(rhs), 0.0).astype(BF16),
                               preferred_element_type=F32).astype(BF16)
    for key in items:
        qo_all[key] = jnp.dot(attn[key], jnp.where(bd_sol, tile(sol_all[key]), 0.0).astype(BF16),
                              preferred_element_type=F32)
    chunk = {}
    for s, c in items:
        d = seqs[s]
        sl = rows[s, c]
        for h in range(GDN_HEADS):
            hl = slice(h * GDN_HEAD, (h + 1) * GDN_HEAD)
            uw = slice(2 * h * GDN_HEAD, 2 * (h + 1) * GDN_HEAD)
            qo = qo_all[s, c][:, uw]
            mn = lax.dot_general(d["k_dec"][sl, hl], sol_all[s, c][:, uw], (((0,), (0,)), ((), ())),
                                 preferred_element_type=F32)
            chunk[s, c, h] = dict(
                m_qd=jnp.concatenate([mn[:, GDN_HEAD:2 * GDN_HEAD],
                                      d["q_dec"][sl, hl] - qo[:, GDN_HEAD:2 * GDN_HEAD]], axis=0).astype(BF16),
                o0=qo[:, 0:GDN_HEAD], n0=mn[:, 0:GDN_HEAD], c_dec=d["c_dec"][c * CHUNK:c * CHUNK + 1, hl])

    states = {(s, h): s_ref[s, h] for s in range(MIX_NB) for h in range(GDN_HEADS)}
    outs = {}
    for c in range(n_chunks):
        for s in range(MIX_NB):
            for h in range(GDN_HEADS):
                d = chunk[s, c, h]
                st = states[s, h]
                prod = jnp.dot(d["m_qd"], st.astype(BF16), preferred_element_type=F32)
                outs[s, c, h] = prod[GDN_HEAD:GDN_HEAD + CHUNK] + d["o0"]
                states[s, h] = st * d["c_dec"] - prod[0:GDN_HEAD] + d["n0"]
    for s in range(MIX_NB):
        for h in range(GDN_HEADS):
            s_ref[s, h] = states[s, h]

    for s in range(MIX_NB):
        o = jnp.concatenate([jnp.concatenate([outs[s, c, h] for h in range(GDN_HEADS)], axis=1)
                             for c in range(n_chunks)], axis=0)
        o_ref[s] = (o * lax.rsqrt(_head_sumsq(o, GDN_HEAD) * (1.0 / GDN_HEAD) + RMS_EPS) * vec_ref[2:3, :]
                    * gate_ref[s])


def _gdn(qkv, gate, small, l, vecs):
    bsz, t_len, _ = qkv.shape
    return pl.pallas_call(
        _gdn_body,
        grid=(bsz // MIX_NB, t_len // MIX_TB),
        in_specs=[_mix_spec(_QKV_W), _mix_spec(GDN_WIDTH), _mix_spec(P_SMALL), _pick(vecs, l)],
        out_specs=_mix_spec(GDN_WIDTH),
        out_shape=jax.ShapeDtypeStruct((bsz, t_len, GDN_WIDTH), F32),
        scratch_shapes=[pltpu.VMEM((MIX_NB, GDN_HEADS, GDN_HEAD, GDN_HEAD), F32)],
        compiler_params=pltpu.CompilerParams(dimension_semantics=("parallel", "arbitrary"),
                                             vmem_limit_bytes=VMEM_LIMIT),
        name="gdn",
    )(qkv, gate, small, vecs)


def _zeros_like_lead(ref, *shape):
    return jnp.zeros((ref.shape[0],) + shape, F32)


def _proj_weights(w_in, w_in_vres):
    o_s5 = RW_COLS
    o_qkv = o_s5 + S5_WIDTH
    o_gate = o_qkv + _QKV_W
    o_ba = o_gate + GDN_WIDTH
    w_in = w_in.astype(F32)
    vres = jnp.concatenate([jnp.zeros((1,) + w_in_vres.shape[1:], F32), w_in_vres.astype(F32)], axis=0)
    used = 2 * GDN_HEADS + RW_V_RANK
    return jnp.concatenate(
        [w_in[:, :, o_qkv:o_gate], w_in[:, :, 0:RW_COLS], w_in[:, :, o_gate:o_ba], w_in[:, :, o_s5:o_qkv],
         w_in[:, :, o_ba:o_ba + 2 * GDN_HEADS], vres, _zeros_like_lead(w_in, D_MODEL, P_SMALL - used)],
        axis=2).astype(BF16)


def _s5_matrices(abar_re, abar_im, bb_re, bb_im, c_re, c_im):
    g, n, hc = S5_GROUPS, S5_STATE, S5_GROUP_DIM
    depth = c_re.shape[0]
    eye = jnp.eye(g, dtype=F32)
    place_b = lambda bb: jnp.einsum("lgnh,gk->lghkn", bb.reshape(depth, g, n, hc), eye).reshape(depth, g * hc, g * n)
    bmat = jnp.concatenate([place_b(bb_re), place_b(bb_im)], axis=2)
    place_c = lambda cc: jnp.einsum("lghn,gk->lgnkh", cc.astype(F32), eye).reshape(depth, g * n, g * hc)
    cmat = jnp.concatenate([place_c(c_re), -place_c(c_im)], axis=1)
    abar = jnp.concatenate([abar_re.reshape(depth, 1, g * n), abar_im.reshape(depth, 1, g * n)], axis=1)
    return abar, bmat.astype(BF16), cmat.astype(BF16)


def _row_stack(rows, n_rows):
    rows = [r.astype(F32)[:, None, :] for r in rows]
    pad = jnp.zeros((rows[0].shape[0], n_rows - len(rows), rows[0].shape[2]), F32)
    return jnp.concatenate(rows + [pad], axis=1)


def kernel(x, norm_gain, ffn_w1, ffn_w3, ffn_w2, w_in, w_in_vres, w_out, rwkv_mu, rwkv_w0, rwkv_w_w2, rwkv_a0, rwkv_w_a2, rwkv_w_g2, rwkv_k_k, rwkv_k_a, rwkv_r_k, rwkv_ln_w, rwkv_ln_b, rwkv_v0, rwkv_w_v2, s5_a_re, s5_a_im, s5_log_dt, s5_b_re, s5_b_im, s5_c_re, s5_c_im, s5_d, s5_w_glu, s5_b_glu, gdn_conv_w, gdn_a_log, gdn_dt_bias, gdn_norm_w):
    depth = norm_gain.shape[0]
    gains = norm_gain.astype(F32)
    w1 = ffn_w1.astype(BF16)
    w3 = ffn_w3.astype(BF16)
    w2 = ffn_w2.astype(BF16)
    wo = w_out.astype(BF16)
    w_proj = _proj_weights(w_in, w_in_vres)
    lead0 = lambda z: jnp.concatenate([jnp.zeros((1,) + z.shape[1:], F32), z.astype(F32)], axis=0)
    pad_to = lambda z, width: jnp.pad(z.astype(F32), ((0, 0), (0, width - z.shape[1])))
    pvec = jnp.concatenate([gdn_conv_w.astype(F32), pad_to(rwkv_mu, _QKV_W)[:, None, :],
                            _zeros_like_lead(gdn_conv_w, 8 - GDN_CONV - 1, _QKV_W)], axis=1)
    zero = _zeros_like_lead(rwkv_w0, RW_WIDTH)
    rw_vecs = _row_stack([rwkv_w0, rwkv_a0, rwkv_k_k, rwkv_k_a, rwkv_r_k.reshape(depth, RW_WIDTH), rwkv_ln_w,
                          rwkv_ln_b, zero, lead0(rwkv_v0)], 16)
    z64 = _zeros_like_lead(rwkv_w_w2, 64, RW_WIDTH)
    wlr = jnp.concatenate([jnp.concatenate([rwkv_w_w2.astype(F32), z64], axis=2),
                           jnp.concatenate([z64, rwkv_w_a2.astype(F32)], axis=2)], axis=1).astype(BF16)
    wg2 = rwkv_w_g2.astype(BF16)
    wv2 = jnp.concatenate([_zeros_like_lead(rwkv_w0, 2 * GDN_HEADS, RW_WIDTH), lead0(rwkv_w_v2),
                           _zeros_like_lead(rwkv_w0, P_SMALL - 2 * GDN_HEADS - RW_V_RANK, RW_WIDTH)],
                          axis=1).astype(BF16)
    abar_re, abar_im, bb_re, bb_im = _s5_discretize(s5_a_re, s5_a_im, s5_log_dt, s5_b_re, s5_b_im)
    abar, bmat, cmat = _s5_matrices(abar_re, abar_im, bb_re, bb_im, s5_c_re, s5_c_im)
    s5_vecs = _row_stack([s5_d, s5_b_glu], 8)
    wglu = s5_w_glu.astype(BF16)
    compact = lambda z: jnp.pad(z.astype(F32), ((0, 0), (GDN_HEADS, GDN_WIDTH - 2 * GDN_HEADS)))
    gdn_vecs = _row_stack([compact(gdn_a_log), compact(gdn_dt_bias), jnp.tile(gdn_norm_w, (1, GDN_HEADS))], 8)

    x = x.astype(F32)
    v_first = None
    for l in range(depth):
        x, p_qkv, p_rw, p_gate, p_s5, p_small = _ffn_inproj(x, l, gains, w1, w3, w2, w_proj, pvec)
        if l == 0:
            y_a, v_first = _rwkv(p_rw, None, None, l, rw_vecs, wlr, wg2, None)
        else:
            y_a = _rwkv(p_rw, p_small, v_first, l, rw_vecs, wlr, wg2, wv2)
        y_b = _s5(p_s5, l, abar, bmat, cmat, s5_vecs, wglu)
        y_c = _gdn(p_qkv, p_gate, p_small, l, gdn_vecs)
        x = _mixout_ffn(x, y_a, y_b, y_c, l, gains, wo, w1, w3, w2)
    return x
```

```python
import functools
import math

import jax
import jax.numpy as jnp
from jax import lax
from jax.experimental import pallas as pl
from jax.experimental.pallas import tpu as pltpu

F32 = jnp.float32
BF16 = jnp.bfloat16

D_MODEL = 1024
D_FF = 2816
RMS_EPS = 1e-6
L2_EPS = 1e-12

RW_WIDTH = 256
RW_HEAD = 64
RW_HEADS = 4
RW_GN_EPS = 64e-5
RW_COLS = 1024
RW_V_RANK = 32

S5_WIDTH = 256
S5_GROUPS = 16
S5_GROUP_DIM = 16
S5_STATE = 64
S5_NSTATE = S5_GROUPS * S5_STATE

GDN_WIDTH = 512
GDN_HEAD = 128
GDN_HEADS = 4
GDN_CONV = 4

P_SMALL = 128

CHUNK = 64
MIX_TB = 256
MIX_NB = 4
S5_TB = 256
S5_ROWS = 256
FFN_TM = 512
FFN_TF = 256
VMEM_LIMIT = 56 * 1024 * 1024


def _bdot(a, b):
    return jnp.dot(a.astype(BF16), b.astype(BF16), preferred_element_type=F32)


def _bdot_tn(a, b):
    return lax.dot_general(a.astype(BF16), b.astype(BF16), (((0,), (0,)), ((), ())),
                           preferred_element_type=F32)


def _split(x, n):
    pieces = []
    r = x
    for i in range(n):
        p = r.astype(BF16)
        pieces.append(p)
        if i + 1 < n:
            r = r - p.astype(F32)
    return pieces


def _dot_exact_rhs(w01, x, n=2):
    out = None
    for p in _split(x, n):
        t = jnp.dot(w01, p, preferred_element_type=F32)
        out = t if out is None else out + t
    return out


def _head_sums(x, head):
    assert 2 * head == 128
    low = _iota((1, 128), 1) < head
    parts = []
    for c in range(x.shape[1] // 128):
        xc = x[:, c * 128:(c + 1) * 128]
        s_lo = jnp.sum(jnp.where(low, xc, 0.0), axis=-1, keepdims=True)
        s_hi = jnp.sum(jnp.where(low, 0.0, xc), axis=-1, keepdims=True)
        parts.append(jnp.where(low, s_lo, s_hi))
    return jnp.concatenate(parts, axis=1)


def _rms(z, gain):
    ms = jnp.mean(z * z, axis=-1, keepdims=True)
    return z * lax.rsqrt(ms + RMS_EPS) * gain


def _sigmoid(x):
    return 1.0 / (1.0 + jnp.exp2(x * (-1.0 / math.log(2.0))))


def _softplus(x):
    return jnp.maximum(x, 0.0) + jnp.log1p(jnp.exp(-jnp.abs(x)))


def _iota(shape, dim):
    return lax.broadcasted_iota(jnp.int32, shape, dim)


def _chunk_lower(tb):
    r = _iota((tb, tb), 0)
    c = _iota((tb, tb), 1)
    return jnp.where(((r // CHUNK) == (c // CHUNK)) & (c <= r), 1.0, 0.0).astype(BF16)


def _chunk_last_rows(x):
    parts = []
    for c in range(x.shape[0] // CHUNK):
        last = x[(c + 1) * CHUNK - 1:(c + 1) * CHUNK, :]
        parts.append(jnp.broadcast_to(last, (CHUNK, x.shape[1])))
    return jnp.concatenate(parts, axis=0)


def _block_ones(n, blk):
    r = _iota((n, n), 0)
    c = _iota((n, n), 1)
    return (r // blk) == (c // blk)


def _head_block_diag(x, bdm):
    reps = bdm.shape[0] // x.shape[0]
    return jnp.where(bdm, jnp.concatenate([x] * reps, axis=0), 0.0).astype(BF16)


def _unit_lower_inverse_cat(ns, bdm):
    width = ns[0].shape[1]
    eye = jnp.where(_iota((CHUNK, width), 1) % CHUNK == _iota((CHUNK, width), 0), 1.0, 0.0).astype(F32)
    ts = [eye + n for n in ns]
    ps = [jnp.dot(n.astype(BF16), _head_block_diag(n, bdm), preferred_element_type=F32) for n in ns]
    levels = int(math.log2(CHUNK)) - 1
    for lvl in range(levels):
        new_ts, new_ps = [], []
        for t, p in zip(ts, ps):
            w = _head_block_diag(p, bdm)
            if lvl == levels - 1:
                new_ts.append(t + jnp.dot(t.astype(BF16), w, preferred_element_type=F32))
            else:
                both = jnp.dot(jnp.concatenate([t, p], axis=0).astype(BF16), w, preferred_element_type=F32)
                new_ts.append(t + both[0:CHUNK])
                new_ps.append(both[CHUNK:2 * CHUNK])
        ts, ps = new_ts, new_ps
    return ts


def _pick(arr, *lead):
    rest = arr.shape[len(lead):]
    return pl.BlockSpec((None,) * len(lead) + rest, lambda *_: tuple(lead) + (0,) * len(rest),
                        pipeline_mode=pl.Buffered(1))


_QKV_W = 3 * GDN_WIDTH
_PROJ_WIDTHS = (_QKV_W, RW_COLS, GDN_WIDTH, S5_WIDTH, P_SMALL)


def _head_sumsq(z, head):
    parts = []
    for h in range(z.shape[1] // head):
        zh = z[:, h * head:(h + 1) * head]
        parts.append(jnp.broadcast_to(jnp.sum(zh * zh, axis=-1, keepdims=True), zh.shape))
    return jnp.concatenate(parts, axis=1)


def _l2_heads(z, head):
    return z * lax.rsqrt(jnp.maximum(_head_sumsq(z, head), L2_EPS * L2_EPS))


def _ffn_core(x, gi, go, w1_ref, w3_ref, w2_ref):
    h = _rms(x, gi).astype(BF16)
    acc = jnp.zeros(x.shape, F32)
    for f in range(D_FF // FFN_TF):
        cs = slice(f * FFN_TF, (f + 1) * FFN_TF)
        a = jnp.dot(h, w1_ref[:, cs], preferred_element_type=F32)
        b = jnp.dot(h, w3_ref[:, cs], preferred_element_type=F32)
        act = (a * _sigmoid(a) * b).astype(BF16)
        acc = acc + jnp.dot(act, w2_ref[cs, :], preferred_element_type=F32)
    return x + 0.5 * _rms(acc, go)


def _ffn_inproj_body(x_ref, g_ref, w1_ref, w3_ref, w2_ref, wp_ref, pv_ref, o_ref,
                     qkv_ref, rw_ref, gate_ref, s5_ref, small_ref, cq_ref, cr_ref):
    tm = FFN_TM

    @pl.when(pl.program_id(1) == 0)
    def _():
        cq_ref[...] = jnp.zeros(cq_ref.shape, F32)
        cr_ref[...] = jnp.zeros(cr_ref.shape, F32)

    x = _ffn_core(x_ref[...], g_ref[0:1, :], g_ref[1:2, :], w1_ref, w3_ref, w2_ref)
    o_ref[...] = x
    h = _rms(x, g_ref[2:3, :]).astype(BF16)
    offs = [sum(_PROJ_WIDTHS[:i]) for i in range(len(_PROJ_WIDTHS) + 1)]
    proj = lambda i: jnp.dot(h, wp_ref[:, offs[i]:offs[i + 1]], preferred_element_type=F32)

    qkv = proj(0)
    ext = jnp.concatenate([cq_ref[...], qkv], axis=0)
    cq_ref[...] = qkv[tm - 8:tm, :]
    conv = qkv * pv_ref[GDN_CONV - 1:GDN_CONV, :]
    for j in range(1, GDN_CONV):
        conv = conv + ext[8 - j:8 - j + tm, :] * pv_ref[GDN_CONV - 1 - j:GDN_CONV - j, :]
    act = conv * _sigmoid(conv)
    qkv_ref[:, 0:GDN_WIDTH] = _l2_heads(act[:, 0:GDN_WIDTH], GDN_HEAD) * (GDN_HEAD ** -0.5)
    qkv_ref[:, GDN_WIDTH:2 * GDN_WIDTH] = _l2_heads(act[:, GDN_WIDTH:2 * GDN_WIDTH], GDN_HEAD)
    qkv_ref[:, 2 * GDN_WIDTH:3 * GDN_WIDTH] = act[:, 2 * GDN_WIDTH:3 * GDN_WIDTH]

    rw = proj(1)
    prev = jnp.concatenate([cr_ref[...], rw], axis=0)[7:7 + tm, :]
    cr_ref[...] = rw[tm - 8:tm, :]
    rw_ref[...] = rw + pv_ref[GDN_CONV:GDN_CONV + 1, 0:RW_COLS] * (prev - rw)

    gate = proj(2)
    gate_ref[...] = gate * _sigmoid(gate)
    s5_ref[...] = proj(3)
    small_ref[...] = proj(4)


def _mixout_ffn_body(x_ref, ya_ref, yb_ref, yc_ref, g_ref, wo_ref, w1_ref, w3_ref, w2_ref, o_ref):
    mixed = (jnp.dot(ya_ref[...].astype(BF16), wo_ref[0:256, :], preferred_element_type=F32)
             + jnp.dot(yb_ref[...].astype(BF16), wo_ref[256:512, :], preferred_element_type=F32)
             + jnp.dot(yc_ref[...].astype(BF16), wo_ref[512:1024, :], preferred_element_type=F32))
    x = x_ref[...] + _rms(mixed, g_ref[3:4, :])
    o_ref[...] = _ffn_core(x, g_ref[4:5, :], g_ref[5:6, :], w1_ref, w3_ref, w2_ref)


def _row_spec(width):
    return pl.BlockSpec((None, FFN_TM, width), lambda b, t: (b, t, 0))


def _ffn_inproj(x, l, gains, w1, w3, w2, wp, pvec):
    bsz, t_len, _ = x.shape
    sds = lambda wd: jax.ShapeDtypeStruct((bsz, t_len, wd), F32)
    return pl.pallas_call(
        _ffn_inproj_body,
        grid=(bsz, t_len // FFN_TM),
        in_specs=[_row_spec(D_MODEL), _pick(gains, l), _pick(w1, l, 0), _pick(w3, l, 0), _pick(w2, l, 0),
                  _pick(wp, l), _pick(pvec, l)],
        out_specs=[_row_spec(D_MODEL)] + [_row_spec(wd) for wd in _PROJ_WIDTHS],
        out_shape=[sds(D_MODEL)] + [sds(wd) for wd in _PROJ_WIDTHS],
        scratch_shapes=[pltpu.VMEM((8, _QKV_W), F32), pltpu.VMEM((8, RW_COLS), F32)],
        compiler_params=pltpu.CompilerParams(dimension_semantics=("parallel", "arbitrary"),
                                             vmem_limit_bytes=VMEM_LIMIT),
        name="ffn_inproj",
    )(x, gains, w1, w3, w2, wp, pvec)


def _mixout_ffn(x, ya, yb, yc, l, gains, wo, w1, w3, w2):
    bsz, t_len, _ = x.shape
    return pl.pallas_call(
        _mixout_ffn_body,
        grid=(bsz, t_len // FFN_TM),
        in_specs=[_row_spec(D_MODEL), _row_spec(RW_WIDTH), _row_spec(S5_WIDTH), _row_spec(GDN_WIDTH),
                  _pick(gains, l), _pick(wo, l), _pick(w1, l, 1), _pick(w3, l, 1), _pick(w2, l, 1)],
        out_specs=_row_spec(D_MODEL),
        out_shape=jax.ShapeDtypeStruct(x.shape, F32),
        compiler_params=pltpu.CompilerParams(dimension_semantics=("parallel", "parallel"),
                                             vmem_limit_bytes=VMEM_LIMIT),
        name="mixout_ffn",
    )(x, ya, yb, yc, gains, wo, w1, w3, w2)


def _mix_spec(width):
    return pl.BlockSpec((MIX_NB, MIX_TB, width), lambda b, t: (b, t, 0))


def _rwkv_body(has_vres, *refs):
    if has_vres:
        (p_ref, small_ref, vfirst_ref, vec_ref, wlr_ref, wg2_ref, wv2_ref, y_ref, h_ref) = refs
    else:
        (p_ref, vec_ref, wlr_ref, wg2_ref, y_ref, vout_ref, h_ref) = refs
    tb = MIX_TB
    width = RW_WIDTH
    n_chunks = tb // CHUNK

    @pl.when(pl.program_id(1) == 0)
    def _():
        h_ref[...] = jnp.zeros(h_ref.shape, F32)

    vec = lambda i: vec_ref[i:i + 1, :]
    lower = _chunk_lower(tb)
    ri = _iota((CHUNK, width), 0)
    ci = _iota((CHUNK, width), 1) % CHUNK
    strict = ci < ri
    incl = ci <= ri
    bdm = _block_ones(width, RW_HEAD)
    bd = lambda x: _head_block_diag(x, bdm)

    seqs = []
    for s in range(MIX_NB):
        z = p_ref[s]
        r = z[:, 0:256]
        k = z[:, 256:512]
        v = z[:, 512:768]
        lr_in = z[:, 768:896]
        lr_act = jnp.where(_iota((1, 128), 1) < 64, jnp.tanh(lr_in), lr_in)
        lr = _bdot(lr_act, wlr_ref[...])
        lw = -math.exp(-0.5) * _sigmoid(vec(0) + lr[:, 0:256])
        a = _sigmoid(vec(1) + lr[:, 256:512])
        g = _bdot(_sigmoid(z[:, 896:1024]), wg2_ref[...])
        if has_vres:
            mix = _sigmoid(vec(8) + _bdot(small_ref[s], wv2_ref[...]))
            v = v + (vfirst_ref[s] - v) * mix
        else:
            vout_ref[s] = v
        kkr = k * vec(2)
        kk = kkr * lax.rsqrt(jnp.maximum(_head_sums(kkr * kkr, RW_HEAD), L2_EPS * L2_EPS))
        k2 = k * (1.0 + (a - 1.0) * vec(3))
        bv = kk * a
        cl = _dot_exact_rhs(lower, lw)
        cl_end = _chunk_last_rows(cl)
        e_inv = jnp.exp(-cl)
        e_end = jnp.exp(cl_end - cl)
        seqs.append(dict(
            r=r, v=v, g=g, k2=k2,
            rt=r * jnp.exp(cl), at=-kk * jnp.exp(cl - lw), bt=bv * e_inv, kt=k2 * e_inv,
            bh=bv * e_end, kh=k2 * e_end, g_end=jnp.exp(cl_end)))

    items = [(s, c) for s in range(MIX_NB) for c in range(n_chunks)]
    pb, pk = {}, {}
    for s, c in items:
        q = seqs[s]
        sl = slice(c * CHUNK, (c + 1) * CHUNK)
        lhs = jnp.concatenate([q["at"][sl], q["rt"][sl]], axis=0).astype(BF16)
        pb[s, c] = lax.dot_general(lhs, bd(q["bt"][sl]), (((1,), (1,)), ((), ())), preferred_element_type=F32)
        pk[s, c] = lax.dot_general(lhs, bd(q["kt"][sl]), (((1,), (1,)), ((), ())), preferred_element_type=F32)
    tinv = dict(zip(items, _unit_lower_inverse_cat(
        [jnp.where(strict, pb[key][0:CHUNK], 0.0) for key in items], bdm)))
    rows = {(s, c): slice(c * CHUNK, (c + 1) * CHUNK) for s, c in items}
    sub = lambda key, name: seqs[key[0]][name][rows[key]]
    akv = {}
    for key in items:
        a_k = jnp.concatenate([jnp.where(strict, pk[key][0:CHUNK], 0.0),
                               jnp.where(incl, pk[key][CHUNK:2 * CHUNK], 0.0)], axis=0).astype(BF16)
        akv[key] = jnp.dot(a_k, bd(sub(key, "v")), preferred_element_type=F32)
    wu = {key: jnp.dot(tinv[key].astype(BF16),
                       jnp.concatenate([bd(sub(key, "at")), bd(akv[key][0:CHUNK])], axis=1),
                       preferred_element_type=F32) for key in items}
    rwu = {key: jnp.dot(jnp.where(incl, pb[key][CHUNK:2 * CHUNK], 0.0).astype(BF16),
                        jnp.concatenate([bd(wu[key][:, 0:width]), bd(wu[key][:, width:2 * width])], axis=1),
                        preferred_element_type=F32) for key in items}
    chunk = {}
    for key in items:
        wt, ut = wu[key][:, 0:width], wu[key][:, width:2 * width]
        m = jnp.where(bdm, _bdot_tn(sub(key, "bh"), wt), 0.0)
        chunk[key] = dict(
            m_ry=jnp.concatenate([m, sub(key, "rt") + rwu[key][:, 0:width]], axis=0).astype(BF16),
            y0=rwu[key][:, width:2 * width] + akv[key][CHUNK:2 * CHUNK],
            n0=jnp.where(bdm, _bdot_tn(jnp.concatenate([sub(key, "bh"), sub(key, "kh")], axis=0),
                                       jnp.concatenate([ut, sub(key, "v")], axis=0)), 0.0),
            gcol=jnp.transpose(jnp.concatenate([sub(key, "g_end")] * (width // CHUNK), axis=0)))

    states = [h_ref[s] for s in range(MIX_NB)]
    ys = [[] for _ in range(MIX_NB)]
    for c in range(n_chunks):
        for s in range(MIX_NB):
            d = chunk[s, c]
            prod = jnp.dot(d["m_ry"], states[s].astype(BF16), preferred_element_type=F32)
            ys[s].append(prod[width:width + CHUNK] + d["y0"])
            states[s] = d["gcol"] * states[s] + prod[0:width] + d["n0"]
    for s in range(MIX_NB):
        h_ref[s] = states[s]

    inv_n = 1.0 / RW_HEAD
    for s in range(MIX_NB):
        q = seqs[s]
        y = jnp.concatenate(ys[s], axis=0)
        yc = y - _head_sums(y, RW_HEAD) * inv_n
        var = _head_sums(yc * yc, RW_HEAD) * inv_n
        yn = yc * lax.rsqrt(var + RW_GN_EPS) * vec(5) + vec(6)
        yn = yn + _head_sums(q["r"] * q["k2"] * vec(4), RW_HEAD) * q["v"]
        y_ref[s] = yn * q["g"]


def _rwkv(p_rw, small, v_first, l, vecs, wlr, wg2, wv2):
    bsz, t_len, _ = p_rw.shape
    has_vres = v_first is not None
    y_sds = jax.ShapeDtypeStruct((bsz, t_len, RW_WIDTH), F32)
    if has_vres:
        args = (p_rw, small, v_first, vecs, wlr, wg2, wv2)
        in_specs = [_mix_spec(RW_COLS), _mix_spec(P_SMALL), _mix_spec(RW_WIDTH)] + [
            _pick(a, l) for a in (vecs, wlr, wg2, wv2)]
        out_specs, out_shape = _mix_spec(RW_WIDTH), y_sds
    else:
        args = (p_rw, vecs, wlr, wg2)
        in_specs = [_mix_spec(RW_COLS)] + [_pick(a, l) for a in (vecs, wlr, wg2)]
        out_specs, out_shape = [_mix_spec(RW_WIDTH), _mix_spec(RW_WIDTH)], [y_sds, y_sds]
    return pl.pallas_call(
        functools.partial(_rwkv_body, has_vres),
        grid=(bsz // MIX_NB, t_len // MIX_TB),
        in_specs=in_specs,
        out_specs=out_specs,
        out_shape=out_shape,
        scratch_shapes=[pltpu.VMEM((MIX_NB, RW_WIDTH, RW_WIDTH), F32)],
        compiler_params=pltpu.CompilerParams(dimension_semantics=("parallel", "arbitrary"),
                                             vmem_limit_bytes=VMEM_LIMIT),
        name="rwkv_vres" if has_vres else "rwkv",
    )(*args)


def _s5_disc_body(are_ref, aim_ref, ldt_ref, bre_ref, bim_ref, abar_re_ref, abar_im_ref, bbre_ref, bbim_ref):
    a_re = are_ref[...]
    a_im = aim_ref[...]
    dt = jnp.exp(ldt_ref[...])
    mag = jnp.exp(dt * a_re)
    ang = dt * a_im
    abar_re = mag * jnp.cos(ang)
    abar_im = mag * jnp.sin(ang)
    den = a_re * a_re + a_im * a_im
    num_re = abar_re - 1.0
    coef_re = (num_re * a_re + abar_im * a_im) / den
    coef_im = (abar_im * a_re - num_re * a_im) / den
    abar_re_ref[...] = abar_re
    abar_im_ref[...] = abar_im
    b_re = bre_ref[...]
    b_im = bim_ref[...]
    bbre_ref[...] = coef_re * b_re - coef_im * b_im
    bbim_ref[...] = coef_re * b_im + coef_im * b_re


def _s5_discretize(a_re, a_im, log_dt, b_re, b_im):
    n = a_re.shape[0] * S5_NSTATE
    col = lambda z: z.astype(F32).reshape(n, 1)
    ldt = jnp.broadcast_to(log_dt.astype(F32)[:, :, None], a_re.shape)
    mat = lambda z: z.astype(F32).reshape(n, S5_GROUP_DIM)
    return pl.pallas_call(
        _s5_disc_body,
        out_shape=[jax.ShapeDtypeStruct((n, 1), F32)] * 2 + [jax.ShapeDtypeStruct((n, S5_GROUP_DIM), F32)] * 2,
        name="s5_discretize",
    )(col(a_re), col(a_im), col(ldt), mat(b_re), mat(b_im))


def _gelu_tanh(x):
    return 0.5 * x * (1.0 + jnp.tanh(math.sqrt(2.0 / math.pi) * (x + 0.044715 * (x * x * x))))


def _s5_body(u_ref, abar_ref, bmat_ref, cmat_ref, vec_ref, wglu_ref, y_ref, x_ref, tm_ref, st_ref):
    n = S5_NSTATE
    nb = u_ref.shape[0]
    halves = S5_WIDTH // 128

    @pl.when(pl.program_id(0) == 0)
    def _():
        st_ref[...] = jnp.zeros(st_ref.shape, F32)

    for b in range(nb):
        ub = u_ref[b]
        for j in range(halves):
            tm_ref[j, pl.ds(b, S5_TB, stride=nb), :] = ub[:, j * 128:(j + 1) * 128]
    u = jnp.concatenate([tm_ref[j] for j in range(halves)], axis=1)
    n_rows = S5_TB * nb
    row_blocks = [slice(r, r + S5_ROWS) for r in range(0, n_rows, S5_ROWS)]
    for rs in row_blocks:
        x_ref[rs, :] = _bdot(u[rs], bmat_ref[...])
    a_re = jnp.broadcast_to(abar_ref[0:1, :], (nb, n))
    a_im = jnp.broadcast_to(abar_ref[1:2, :], (nb, n))

    def step(t, carry):
        s_re, s_im = carry
        rows = pl.ds(pl.multiple_of(t * nb, nb), nb)
        n_re = a_re * s_re - a_im * s_im + x_ref[rows, 0:n]
        n_im = a_re * s_im + a_im * s_re + x_ref[rows, n:2 * n]
        x_ref[rows, 0:n] = n_re
        x_ref[rows, n:2 * n] = n_im
        return n_re, n_im

    s_re, s_im = lax.fori_loop(0, S5_TB, step, (st_ref[:, 0:n], st_ref[:, n:2 * n]))
    st_ref[:, 0:n] = s_re
    st_ref[:, n:2 * n] = s_im

    for rs in row_blocks:
        y = _bdot(x_ref[rs, :], cmat_ref[...]) + vec_ref[0:1, :] * u[rs]
        zz = _gelu_tanh(y)
        res = zz * _sigmoid(_bdot(zz, wglu_ref[...]) + vec_ref[1:2, :])
        for j in range(halves):
            tm_ref[j, rs, :] = res[:, j * 128:(j + 1) * 128]
    for b in range(nb):
        y_ref[b] = jnp.concatenate([tm_ref[j, pl.ds(b, S5_TB, stride=nb), :] for j in range(halves)], axis=1)


def _s5(u, l, abar, bmat, cmat, vecs, wglu):
    bsz, t_len, _ = u.shape
    blk = pl.BlockSpec((bsz, S5_TB, S5_WIDTH), lambda t: (0, t, 0))
    return pl.pallas_call(
        _s5_body,
        grid=(t_len // S5_TB,),
        in_specs=[blk] + [_pick(a, l) for a in (abar, bmat, cmat, vecs, wglu)],
        out_specs=blk,
        out_shape=jax.ShapeDtypeStruct(u.shape, F32),
        scratch_shapes=[pltpu.VMEM((S5_TB * bsz, 2 * S5_NSTATE), F32),
                        pltpu.VMEM((S5_WIDTH // 128, S5_TB * bsz, 128), F32),
                        pltpu.VMEM((bsz, 2 * S5_NSTATE), F32)],
        compiler_params=pltpu.CompilerParams(dimension_semantics=("arbitrary",),
                                             vmem_limit_bytes=VMEM_LIMIT),
        name="s5",
    )(u, abar, bmat, cmat, vecs, wglu)


def _gdn_body(qkv_ref, gate_ref, small_ref, vec_ref, o_ref, s_ref):
    tb = MIX_TB
    width = GDN_WIDTH
    n_chunks = tb // CHUNK
    cat = GDN_HEADS * CHUNK

    @pl.when(pl.program_id(1) == 0)
    def _():
        s_ref[...] = jnp.zeros(s_ref.shape, F32)

    lane_c = _iota((1, P_SMALL), 1)
    bc = lambda z, j: jnp.broadcast_to(z[:, j:j + 1], (tb, GDN_HEAD))
    lower = _chunk_lower(tb)
    upper = (_iota((tb, cat), 0) % CHUNK) > (_iota((tb, cat), 1) % CHUNK)
    ri = _iota((CHUNK, cat), 0)
    ci = _iota((CHUNK, cat), 1) % CHUNK
    strict = ci < ri
    incl = ci <= ri
    bdm = _block_ones(cat, CHUNK)
    bd_k = (_iota((cat, width), 0) // CHUNK) == (_iota((cat, width), 1) // GDN_HEAD)
    bd_sol = (_iota((cat, 2 * width), 0) // CHUNK) == (_iota((cat, 2 * width), 1) // (2 * GDN_HEAD))
    tile = lambda x: jnp.concatenate([x] * GDN_HEADS, axis=0)

    seqs = []
    for s in range(MIX_NB):
        x = qkv_ref[s]
        q = x[:, 0:width]
        k = x[:, width:2 * width]
        v = x[:, 2 * width:3 * width]
        cs = small_ref[s]
        gate_c = jnp.where(lane_c < GDN_HEADS, _sigmoid(cs),
                           -jnp.exp(vec_ref[0:1, 0:P_SMALL]) * _softplus(cs + vec_ref[1:2, 0:P_SMALL]))
        gam_c = _dot_exact_rhs(lower, gate_c)
        beta = jnp.concatenate([bc(gate_c, h) for h in range(GDN_HEADS)], axis=1)
        gam = jnp.concatenate([bc(gam_c, GDN_HEADS + h) for h in range(GDN_HEADS)], axis=1)
        g_cat = jnp.concatenate([jnp.where(lane_c < CHUNK, bc(gate_c, GDN_HEADS + h), bc(gate_c, GDN_HEADS + h + 1))
                                 for h in range(0, GDN_HEADS, 2)], axis=1)
        gam_end = _chunk_last_rows(gam)
        e_gam = jnp.exp(gam)
        k_beta = k * beta
        seqs.append(dict(
            q=q.astype(BF16), k=k.astype(BF16), k_beta=k_beta.astype(BF16),
            dec=jnp.exp(_dot_exact_rhs(lower, jnp.where(upper, g_cat, 0.0))),
            v_beta=(v * beta).astype(BF16), w_rhs=(k_beta * e_gam).astype(BF16),
            q_dec=q * e_gam, k_dec=(k * jnp.exp(gam_end - gam)).astype(BF16), c_dec=jnp.exp(gam_end)))

    items = [(s, c) for s in range(MIX_NB) for c in range(n_chunks)]
    ns, attn = {}, {}
    for s, c in items:
        d = seqs[s]
        sl = slice(c * CHUNK, (c + 1) * CHUNK)
        k_bd = jnp.where(bd_k, tile(d["k"][sl]), 0.0).astype(BF16)
        p1 = lax.dot_general(jnp.concatenate([d["k_beta"][sl], d["q"][sl]], axis=0), k_bd,
                             (((1,), (1,)), ((), ())), preferred_element_type=F32)
        dec = d["dec"][sl]
        ns[s, c] = jnp.where(strict, -(p1[0:CHUNK] * dec), 0.0)
        attn[s, c] = jnp.where(incl, p1[CHUNK:2 * CHUNK] * dec, 0.0).astype(BF16)
    tinv = dict(zip(items, _unit_lower_inverse_cat([ns[key] for key in items], bdm)))
    rows = {(s, c): slice(c * CHUNK, (c + 1) * CHUNK) for s, c in items}
    sol_all, qo_all = {}, {}
    for key in items:
        d = seqs[key[0]]
        rhs = jnp.concatenate([z[rows[key], h * GDN_HEAD:(h + 1) * GDN_HEAD] for h in range(GDN_HEADS)
                               for z in (d["v_beta"], d["w_rhs"])], axis=1)
        sol_all[key] = jnp.dot(tinv[key].astype(BF16), jnp.where(bd_sol, tile(rhs), 0.0).astype(BF16),
                               preferred_element_type=F32).astype(BF16)
    for key in items:
        qo_all[key] = jnp.dot(attn[key], jnp.where(bd_sol, tile(sol_all[key]), 0.0).astype(BF16),
                              preferred_element_type=F32)
    chunk = {}
    for s, c in items:
        d = seqs[s]
        sl = rows[s, c]
        for h in range(GDN_HEADS):
            hl = slice(h * GDN_HEAD, (h + 1) * GDN_HEAD)
            uw = slice(2 * h * GDN_HEAD, 2 * (h + 1) * GDN_HEAD)
            qo = qo_all[s, c][:, uw]
            mn = lax.dot_general(d["k_dec"][sl, hl], sol_all[s, c][:, uw], (((0,), (0,)), ((), ())),
                                 preferred_element_type=F32)
            chunk[s, c, h] = dict(
                m_qd=jnp.concatenate([mn[:, GDN_HEAD:2 * GDN_HEAD],
                                      d["q_dec"][sl, hl] - qo[:, GDN_HEAD:2 * GDN_HEAD]], axis=0).astype(BF16),
                o0=qo[:, 0:GDN_HEAD], n0=mn[:, 0:GDN_HEAD], c_dec=d["c_dec"][c * CHUNK:c * CHUNK + 1, hl])

    states = {(s, h): s_ref[s, h] for s in range(MIX_NB) for h in range(GDN_HEADS)}
    outs = {}
    for c in range(n_chunks):
        for s in range(MIX_NB):
            for h in range(GDN_HEADS):
                d = chunk[s, c, h]
                st = states[s, h]
                prod = jnp.dot(d["m_qd"], st.astype(BF16), preferred_element_type=F32)
                outs[s, c, h] = prod[GDN_HEAD:GDN_HEAD + CHUNK] + d["o0"]
                states[s, h] = st * d["c_dec"] - prod[0:GDN_HEAD] + d["n0"]
    for s in range(MIX_NB):
        for h in range(GDN_HEADS):
            s_ref[s, h] = states[s, h]

    for s in range(MIX_NB):
        o = jnp.concatenate([jnp.concatenate([outs[s, c, h] for h in range(GDN_HEADS)], axis=1)
                             for c in range(n_chunks)], axis=0)
        o_ref[s] = (o * lax.rsqrt(_head_sumsq(o, GDN_HEAD) * (1.0 / GDN_HEAD) + RMS_EPS) * vec_ref[2:3, :]
                    * gate_ref[s])


def _gdn(qkv, gate, small, l, vecs):
    bsz, t_len, _ = qkv.shape
    return pl.pallas_call(
        _gdn_body,
        grid=(bsz // MIX_NB, t_len // MIX_TB),
        in_specs=[_mix_spec(_QKV_W), _mix_spec(GDN_WIDTH), _mix_spec(P_SMALL), _pick(vecs, l)],
        out_specs=_mix_spec(GDN_WIDTH),
        out_shape=jax.ShapeDtypeStruct((bsz, t_len, GDN_WIDTH), F32),
        scratch_shapes=[pltpu.VMEM((MIX_NB, GDN_HEADS, GDN_HEAD, GDN_HEAD), F32)],
        compiler_params=pltpu.CompilerParams(dimension_semantics=("parallel", "arbitrary"),
                                             vmem_limit_bytes=VMEM_LIMIT),
        name="gdn",
    )(qkv, gate, small, vecs)


def _zeros_like_lead(ref, *shape):
    return jnp.zeros((ref.shape[0],) + shape, F32)


def _proj_weights(w_in, w_in_vres):
    o_s5 = RW_COLS
    o_qkv = o_s5 + S5_WIDTH
    o_gate = o_qkv + _QKV_W
    o_ba = o_gate + GDN_WIDTH
    w_in = w_in.astype(F32)
    vres = jnp.concatenate([jnp.zeros((1,) + w_in_vres.shape[1:], F32), w_in_vres.astype(F32)], axis=0)
    used = 2 * GDN_HEADS + RW_V_RANK
    return jnp.concatenate(
        [w_in[:, :, o_qkv:o_gate], w_in[:, :, 0:RW_COLS], w_in[:, :, o_gate:o_ba], w_in[:, :, o_s5:o_qkv],
         w_in[:, :, o_ba:o_ba + 2 * GDN_HEADS], vres, _zeros_like_lead(w_in, D_MODEL, P_SMALL - used)],
        axis=2).astype(BF16)


def _s5_matrices(abar_re, abar_im, bb_re, bb_im, c_re, c_im):
    g, n, hc = S5_GROUPS, S5_STATE, S5_GROUP_DIM
    depth = c_re.shape[0]
    eye = jnp.eye(g, dtype=F32)
    place_b = lambda bb: jnp.einsum("lgnh,gk->lghkn", bb.reshape(depth, g, n, hc), eye).reshape(depth, g * hc, g * n)
    bmat = jnp.concatenate([place_b(bb_re), place_b(bb_im)], axis=2)
    place_c = lambda cc: jnp.einsum("lghn,gk->lgnkh", cc.astype(F32), eye).reshape(depth, g * n, g * hc)
    cmat = jnp.concatenate([place_c(c_re), -place_c(c_im)], axis=1)
    abar = jnp.concatenate([abar_re.reshape(depth, 1, g * n), abar_im.reshape(depth, 1, g * n)], axis=1)
    return abar, bmat.astype(BF16), cmat.astype(BF16)


def _row_stack(rows, n_rows):
    rows = [r.astype(F32)[:, None, :] for r in rows]
    pad = jnp.zeros((rows[0].shape[0], n_rows - len(rows), rows[0].shape[2]), F32)
    return jnp.concatenate(rows + [pad], axis=1)


def kernel(x, norm_gain, ffn_w1, ffn_w3, ffn_w2, w_in, w_in_vres, w_out, rwkv_mu, rwkv_w0, rwkv_w_w2, rwkv_a0, rwkv_w_a2, rwkv_w_g2, rwkv_k_k, rwkv_k_a, rwkv_r_k, rwkv_ln_w, rwkv_ln_b, rwkv_v0, rwkv_w_v2, s5_a_re, s5_a_im, s5_log_dt, s5_b_re, s5_b_im, s5_c_re, s5_c_im, s5_d, s5_w_glu, s5_b_glu, gdn_conv_w, gdn_a_log, gdn_dt_bias, gdn_norm_w):
    depth = norm_gain.shape[0]
    gains = norm_gain.astype(F32)
    w1 = ffn_w1.astype(BF16)
    w3 = ffn_w3.astype(BF16)
    w2 = ffn_w2.astype(BF16)
    wo = w_out.astype(BF16)
    w_proj = _proj_weights(w_in, w_in_vres)
    lead0 = lambda z: jnp.concatenate([jnp.zeros((1,) + z.shape[1:], F32), z.astype(F32)], axis=0)
    pad_to = lambda z, width: jnp.pad(z.astype(F32), ((0, 0), (0, width - z.shape[1])))
    pvec = jnp.concatenate([gdn_conv_w.astype(F32), pad_to(rwkv_mu, _QKV_W)[:, None, :],
                            _zeros_like_lead(gdn_conv_w, 8 - GDN_CONV - 1, _QKV_W)], axis=1)
    zero = _zeros_like_lead(rwkv_w0, RW_WIDTH)
    rw_vecs = _row_stack([rwkv_w0, rwkv_a0, rwkv_k_k, rwkv_k_a, rwkv_r_k.reshape(depth, RW_WIDTH), rwkv_ln_w,
                          rwkv_ln_b, zero, lead0(rwkv_v0)], 16)
    z64 = _zeros_like_lead(rwkv_w_w2, 64, RW_WIDTH)
    wlr = jnp.concatenate([jnp.concatenate([rwkv_w_w2.astype(F32), z64], axis=2),
                           jnp.concatenate([z64, rwkv_w_a2.astype(F32)], axis=2)], axis=1).astype(BF16)
    wg2 = rwkv_w_g2.astype(BF16)
    wv2 = jnp.concatenate([_zeros_like_lead(rwkv_w0, 2 * GDN_HEADS, RW_WIDTH), lead0(rwkv_w_v2),
                           _zeros_like_lead(rwkv_w0, P_SMALL - 2 * GDN_HEADS - RW_V_RANK, RW_WIDTH)],
                          axis=1).astype(BF16)
    abar_re, abar_im, bb_re, bb_im = _s5_discretize(s5_a_re, s5_a_im, s5_log_dt, s5_b_re, s5_b_im)
    abar, bmat, cmat = _s5_matrices(abar_re, abar_im, bb_re, bb_im, s5_c_re, s5_c_im)
    s5_vecs = _row_stack([s5_d, s5_b_glu], 8)
    wglu = s5_w_glu.astype(BF16)
    compact = lambda z: jnp.pad(z.astype(F32), ((0, 0), (GDN_HEADS, GDN_WIDTH - 2 * GDN_HEADS)))
    gdn_vecs = _row_stack([compact(gdn_a_log), compact(gdn_dt_bias), jnp.tile(gdn_norm_w, (1, GDN_HEADS))], 8)

    x = x.astype(F32)
    v_first = None
    for l in range(depth):
        x, p_qkv, p_rw, p_gate, p_s5, p_small = _ffn_inproj(x, l, gains, w1, w3, w2, w_proj, pvec)
        if l == 0:
            y_a, v_first = _rwkv(p_rw, None, None, l, rw_vecs, wlr, wg2, None)
        else:
            y_a = _rwkv(p_rw, p_small, v_first, l, rw_vecs, wlr, wg2, wv2)
        y_b = _s5(p_s5, l, abar, bmat, cmat, s5_vecs, wglu)
        y_c = _gdn(p_qkv, p_gate, p_small, l, gdn_vecs)
        x = _mixout_ffn(x, y_a, y_b, y_c, l, gains, wo, w1, w3, w2)
    return x
```
